```python
import math
import jax
import jax.numpy as jnp
from jax import lax
import numpy as np

D_MODEL = 1024
BATCH = 16
SEQ = 2048
DEPTH = 2

ATT_WIDTH = D_MODEL // 2
HEAD_DIM = 128
ATT_HEADS = ATT_WIDTH // HEAD_DIM
ROT_DIM = HEAD_DIM // 4
ROPE_THETA = 500000.0
MOBA_BLOCK = 256
MOBA_TOPK = 3
Q_CHUNK = 128
POOL_WIDTH = D_MODEL // 2
POOL_WINDOWS = (2, 4, 8, 16)
POOL_GROUPS = len(POOL_WINDOWS)
POOL_GROUP_W = POOL_WIDTH // POOL_GROUPS
EVEN_IN = 4 * ATT_WIDTH + 2 * POOL_WIDTH

S5_WIDTH = D_MODEL // 2
S5_GROUP_IN = 16
S5_GROUPS = S5_WIDTH // S5_GROUP_IN
S5_STATE = 64
CONV_WIDTH = D_MODEL // 2
CONV_K = 31
ODD_IN = 2 * S5_WIDTH + 3 * CONV_WIDTH

N_EVEN = (DEPTH + 1) // 2
N_ODD = DEPTH // 2
EPS = 1e-6
NEG = -1e30

kernel_name = "hybrid_moba_pool_s5_conformer"


def rms_norm(x, g):
    xf = x.astype(jnp.float32)
    y = xf * lax.rsqrt(jnp.mean(xf * xf, axis=-1, keepdims=True) + EPS) * g.astype(jnp.float32)
    return y.astype(x.dtype)


def rope_partial(x):
    s = x.shape[1]
    pos = jnp.arange(s, dtype=jnp.float32)
    inv = jnp.power(ROPE_THETA, -jnp.arange(0, ROT_DIM, 2, dtype=jnp.float32) / ROT_DIM)
    ang = pos[:, None] * inv[None, :]
    cos = jnp.cos(ang)[None, :, None, :]
    sin = jnp.sin(ang)[None, :, None, :]
    xf = x.astype(jnp.float32)
    x1 = xf[..., : ROT_DIM // 2]
    x2 = xf[..., ROT_DIM // 2: ROT_DIM]
    out = jnp.concatenate([x1 * cos - x2 * sin, x2 * cos + x1 * sin, xf[..., ROT_DIM:]], axis=-1)
    return out.astype(x.dtype)


def moba_attention(q, k, v):
    b, s, h, hd = q.shape
    nb = -(-s // MOBA_BLOCK)
    pad = nb * MOBA_BLOCK - s
    kp = jnp.pad(k, ((0, 0), (0, pad), (0, 0), (0, 0)))
    vp = jnp.pad(v, ((0, 0), (0, pad), (0, 0), (0, 0)))
    kb = kp.reshape(b, nb, MOBA_BLOCK, h, hd)
    vb = vp.reshape(b, nb, MOBA_BLOCK, h, hd)
    kbar = jnp.mean(kb.astype(jnp.float32), axis=2)
    gate = jnp.einsum('bshd,bnhd->bshn', q.astype(jnp.float32), kbar)
    qblk = jnp.arange(s) // MOBA_BLOCK
    past = jnp.arange(nb)[None, :] < qblk[:, None]
    gate = jnp.where(past[None, :, None, :], gate, -jnp.inf)
    n_sel = min(MOBA_TOPK, nb)
    _, idx = lax.top_k(gate, n_sel)
    valid = idx < qblk[None, :, None, None]

    nc = s // Q_CHUNK
    q_c = q.reshape(b, nc, Q_CHUNK, h, hd)
    idx_c = idx.reshape(b, nc, Q_CHUNK, h, n_sel)
    valid_c = valid.reshape(b, nc, Q_CHUNK, h, n_sel)
    kh = kb.transpose(0, 3, 1, 2, 4)
    vh = vb.transpose(0, 3, 1, 2, 4)
    scale = 1.0 / math.sqrt(hd)
    h_ar = jnp.arange(h)[None, :, None]

    def per_batch(args):
        qb, ib, mb, khb, vhb = args

        def per_chunk(cargs):
            c, qc, ic, mc = cargs
            kg = khb[h_ar, ic]
            vg = vhb[h_ar, ic]
            blk = (c * Q_CHUNK) // MOBA_BLOCK
            ko = lax.dynamic_index_in_dim(khb, blk, axis=1, keepdims=False)
            vo = lax.dynamic_index_in_dim(vhb, blk, axis=1, keepdims=False)
            s_sel = jnp.einsum('qhd,qhjkd->qhjk', qc, kg).astype(jnp.float32) * scale
            s_sel = jnp.where(mc[..., None], s_sel, NEG)
            s_own = jnp.einsum('qhd,hkd->qhk', qc, ko).astype(jnp.float32) * scale
            qpos = c * Q_CHUNK + jnp.arange(Q_CHUNK)
            kpos = blk * MOBA_BLOCK + jnp.arange(MOBA_BLOCK)
            causal = (kpos[None, :] <= qpos[:, None])[:, None, :]
            s_own = jnp.where(causal, s_own, NEG)
            sc = jnp.concatenate([s_sel.reshape(Q_CHUNK, h, n_sel * MOBA_BLOCK), s_own], axis=-1)
            p = jax.nn.softmax(sc, axis=-1).astype(vhb.dtype)
            p_sel = p[..., : n_sel * MOBA_BLOCK].reshape(Q_CHUNK, h, n_sel, MOBA_BLOCK)
            p_own = p[..., n_sel * MOBA_BLOCK:]
            return (jnp.einsum('qhjk,qhjkd->qhd', p_sel, vg)
                    + jnp.einsum('qhk,hkd->qhd', p_own, vo))

        return lax.map(per_chunk, (jnp.arange(nc), qb, ib, mb))

    out = lax.map(per_batch, (q_c, idx_c, valid_c, kh, vh))
    return out.reshape(b, s, h, hd)


def multiscale_pool(u, pool_w, pool_scale):
    b, s, _ = u.shape
    ug = u.reshape(b, s, POOL_GROUPS, POOL_GROUP_W).astype(jnp.float32)
    cs = jnp.concatenate([jnp.zeros((b, 1, POOL_GROUPS, POOL_GROUP_W), jnp.float32),
                          jnp.cumsum(ug, axis=1)], axis=1)
    t = jnp.arange(s)
    outs = []
    for g, w in enumerate(POOL_WINDOWS):
        lo = jnp.maximum(t + 1 - w, 0)
        win = cs[:, 1:, g] - cs[:, lo, g]
        cnt = (t + 1 - lo).astype(jnp.float32)[None, :, None]
        outs.append(win / cnt - ug[:, :, g])
    m = jnp.stack(outs, axis=2)
    y = jnp.einsum('bsgc,gcd->bsgd', m, pool_w.astype(jnp.float32)).reshape(b, s, POOL_WIDTH)
    return (y * pool_scale.astype(jnp.float32)).astype(u.dtype)


def s5_ssm(u, lam_re, lam_im, log_dt, b_re, b_im, c_re, c_im, d):
    bsz, s, _ = u.shape
    f32 = jnp.float32
    uf = u.reshape(bsz, s, S5_GROUPS, S5_GROUP_IN).astype(f32)
    lr, li = lam_re.astype(f32), lam_im.astype(f32)
    dt = jnp.exp(log_dt.astype(f32))[:, None]
    mag = jnp.exp(lr * dt)
    ab_re = mag * jnp.cos(li * dt)
    ab_im = mag * jnp.sin(li * dt)
    den = lr * lr + li * li
    num_re = ab_re - 1.0
    f_re = (num_re * lr + ab_im * li) / den
    f_im = (ab_im * lr - num_re * li) / den
    br, bi = b_re.astype(f32), b_im.astype(f32)
    bb_re = f_re[..., None] * br - f_im[..., None] * bi
    bb_im = f_re[..., None] * bi + f_im[..., None] * br
    bu_re = jnp.einsum('bsgi,gpi->bsgp', uf, bb_re)
    bu_im = jnp.einsum('bsgi,gpi->bsgp', uf, bb_im)
    a_re = jnp.broadcast_to(ab_re, bu_re.shape)
    a_im = jnp.broadcast_to(ab_im, bu_im.shape)

    def combine(e1, e2):
        a1r, a1i, b1r, b1i = e1
        a2r, a2i, b2r, b2i = e2
        return (a2r * a1r - a2i * a1i,
                a2r * a1i + a2i * a1r,
                a2r * b1r - a2i * b1i + b2r,
                a2r * b1i + a2i * b1r + b2i)

    _, _, xr, xi = lax.associative_scan(combine, (a_re, a_im, bu_re, bu_im), axis=1)
    y = (jnp.einsum('bsgp,gip->bsgi', xr, c_re.astype(f32))
         - jnp.einsum('bsgp,gip->bsgi', xi, c_im.astype(f32)))
    y = y.reshape(bsz, s, S5_WIDTH) + d.astype(f32) * u.astype(f32)
    return y.astype(u.dtype)


def conformer_conv(a, gl, dw, ln_g, ln_b, pw):
    g = a * jax.nn.sigmoid(gl)
    kern = dw.reshape(CONV_K, 1, CONV_WIDTH).astype(g.dtype)
    c = lax.conv_general_dilated(g, kern, window_strides=(1,), padding=[(CONV_K - 1, 0)],
                                 dimension_numbers=('NWC', 'WIO', 'NWC'),
                                 feature_group_count=CONV_WIDTH)
    cf = c.astype(jnp.float32)
    mu = jnp.mean(cf, axis=-1, keepdims=True)
    var = jnp.mean(jnp.square(cf - mu), axis=-1, keepdims=True)
    n = (cf - mu) * lax.rsqrt(var + EPS) * ln_g.astype(jnp.float32) + ln_b.astype(jnp.float32)
    return (jax.nn.silu(n).astype(a.dtype)) @ pw


def even_mixer(h, w_in, pool_w, pool_scale, w_out):
    b, s, _ = h.shape
    z = h @ w_in
    A = ATT_WIDTH
    q, k, v, ga, pu, gb = jnp.split(z, [A, 2 * A, 3 * A, 4 * A, 4 * A + POOL_WIDTH], axis=-1)
    q = rope_partial(q.reshape(b, s, ATT_HEADS, HEAD_DIM))
    k = rope_partial(k.reshape(b, s, ATT_HEADS, HEAD_DIM))
    v = v.reshape(b, s, ATT_HEADS, HEAD_DIM)
    att = moba_attention(q, k, v).reshape(b, s, A) * jax.nn.silu(ga)
    pool = multiscale_pool(pu, pool_w, pool_scale) * jax.nn.silu(gb)
    return jnp.concatenate([att, pool], axis=-1) @ w_out


def odd_mixer(h, w_in, lam_re, lam_im, log_dt, b_re, b_im, c_re, c_im, d,
              glu_w, glu_b, dw, ln_g, ln_b, pw, w_out):
    z = h @ w_in
    Wc, Wd = S5_WIDTH, CONV_WIDTH
    su, gc, ca, cb, gd = jnp.split(z, [Wc, 2 * Wc, 2 * Wc + Wd, 2 * Wc + 2 * Wd], axis=-1)
    y = s5_ssm(su, lam_re, lam_im, log_dt, b_re, b_im, c_re, c_im, d)
    ya, yb = jnp.split(y @ glu_w + glu_b, 2, axis=-1)
    ssm_out = ya * jax.nn.sigmoid(yb) * jax.nn.silu(gc)
    conv_out = conformer_conv(ca, cb, dw, ln_g, ln_b, pw) * jax.nn.silu(gd)
    return jnp.concatenate([ssm_out, conv_out], axis=-1) @ w_out


def setup_inputs(seed: int = 0) -> dict:
    key = jax.random.key(seed)
    ks = jax.random.split(key, 24)
    nrm = jax.random.normal
    f32 = jnp.float32
    D = D_MODEL
    x = nrm(ks[0], (BATCH, SEQ, D), f32)
    pre_norm_g = 1.0 + 0.05 * nrm(ks[1], (DEPTH, D), f32)
    post_norm_g = 1.0 + 0.05 * nrm(ks[2], (DEPTH, D), f32)
    e_w_in = nrm(ks[3], (N_EVEN, D, EVEN_IN), f32) * D ** -0.5
    e_pool_w = nrm(ks[4], (N_EVEN, POOL_GROUPS, POOL_GROUP_W, POOL_GROUP_W), f32) * POOL_GROUP_W ** -0.5
    e_pool_scale = 1.0 + 0.05 * nrm(ks[5], (N_EVEN, POOL_WIDTH), f32)
    e_w_out = nrm(ks[6], (N_EVEN, ATT_WIDTH + POOL_WIDTH, D), f32) * (ATT_WIDTH + POOL_WIDTH) ** -0.5
    o_w_in = nrm(ks[7], (N_ODD, D, ODD_IN), f32) * D ** -0.5
    n_idx = jnp.arange(S5_STATE, dtype=f32)
    o_lam_re = -0.5 + 0.01 * nrm(ks[8], (N_ODD, S5_GROUPS, S5_STATE), f32)
    o_lam_im = math.pi * n_idx[None, None, :] + 0.01 * nrm(ks[9], (N_ODD, S5_GROUPS, S5_STATE), f32)
    o_log_dt = jax.random.uniform(ks[10], (N_ODD, S5_GROUPS), f32, math.log(1e-3), math.log(1e-1))
    o_b_re = nrm(ks[11], (N_ODD, S5_GROUPS, S5_STATE, S5_GROUP_IN), f32) * (2.0 * S5_GROUP_IN) ** -0.5
    o_b_im = nrm(ks[12], (N_ODD, S5_GROUPS, S5_STATE, S5_GROUP_IN), f32) * (2.0 * S5_GROUP_IN) ** -0.5
    o_c_re = nrm(ks[13], (N_ODD, S5_GROUPS, S5_GROUP_IN, S5_STATE), f32) * (2.0 * S5_STATE) ** -0.5
    o_c_im = nrm(ks[14], (N_ODD, S5_GROUPS, S5_GROUP_IN, S5_STATE), f32) * (2.0 * S5_STATE) ** -0.5
    o_d = nrm(ks[15], (N_ODD, S5_WIDTH), f32)
    o_glu_w = nrm(ks[16], (N_ODD, S5_WIDTH, 2 * S5_WIDTH), f32) * S5_WIDTH ** -0.5
    o_glu_b = 0.01 * nrm(ks[17], (N_ODD, 2 * S5_WIDTH), f32)
    o_dw = nrm(ks[18], (N_ODD, CONV_K, CONV_WIDTH), f32) * CONV_K ** -0.5
    o_ln_g = 1.0 + 0.05 * nrm(ks[19], (N_ODD, CONV_WIDTH), f32)
    o_ln_b = 0.01 * nrm(ks[20], (N_ODD, CONV_WIDTH), f32)
    o_pw = nrm(ks[21], (N_ODD, CONV_WIDTH, CONV_WIDTH), f32) * CONV_WIDTH ** -0.5
    o_w_out = nrm(ks[22], (N_ODD, S5_WIDTH + CONV_WIDTH, D), f32) * (S5_WIDTH + CONV_WIDTH) ** -0.5
    return {"x": x, "pre_norm_g": pre_norm_g, "post_norm_g": post_norm_g,
            "e_w_in": e_w_in, "e_pool_w": e_pool_w, "e_pool_scale": e_pool_scale, "e_w_out": e_w_out,
            "o_w_in": o_w_in, "o_lam_re": o_lam_re, "o_lam_im": o_lam_im, "o_log_dt": o_log_dt,
            "o_b_re": o_b_re, "o_b_im": o_b_im, "o_c_re": o_c_re, "o_c_im": o_c_im, "o_d": o_d,
            "o_glu_w": o_glu_w, "o_glu_b": o_glu_b, "o_dw": o_dw, "o_ln_g": o_ln_g, "o_ln_b": o_ln_b,
            "o_pw": o_pw, "o_w_out": o_w_out}


def reference(x, pre_norm_g, post_norm_g, e_w_in, e_pool_w, e_pool_scale, e_w_out,
              o_w_in, o_lam_re, o_lam_im, o_log_dt, o_b_re, o_b_im, o_c_re, o_c_im, o_d,
              o_glu_w, o_glu_b, o_dw, o_ln_g, o_ln_b, o_pw, o_w_out):
    for i in range(DEPTH):
        h = rms_norm(x, pre_norm_g[i])
        j = i // 2
        if i % 2 == 0:
            y = even_mixer(h, e_w_in[j], e_pool_w[j], e_pool_scale[j], e_w_out[j])
        else:
            y = odd_mixer(h, o_w_in[j], o_lam_re[j], o_lam_im[j], o_log_dt[j], o_b_re[j], o_b_im[j],
                          o_c_re[j], o_c_im[j], o_d[j], o_glu_w[j], o_glu_b[j], o_dw[j],
                          o_ln_g[j], o_ln_b[j], o_pw[j], o_w_out[j])
        x = x + rms_norm(y, post_norm_g[i])
    return x
```

```python
import functools
import math

import jax
import jax.numpy as jnp
from jax import lax
from jax.experimental import pallas as pl
from jax.experimental.pallas import tpu as pltpu

F32 = jnp.float32
BF16 = jnp.bfloat16

D_MODEL = 1024
EPS = 1e-6
NEG = -1e30

ATT_WIDTH = 512
HEAD_DIM = 128
ATT_HEADS = 4
ROT_DIM = 32
ROPE_THETA = 500000.0
MOBA_BLOCK = 256
MOBA_TOPK = 3
POOL_WIDTH = 512
POOL_WINDOWS = (2, 4, 8, 16)
POOL_GROUP_W = 128
POOL_CARRY = 16

S5_WIDTH = 512
S5_GROUP_IN = 16
S5_GROUPS = 32
S5_STATE = 64
S5_HALF_GROUPS = 16
S5_HALF_LANES = S5_HALF_GROUPS * S5_STATE
S5_SCAN_LANES = 512
CONV_WIDTH = 512
CONV_K = 31
CONV_HIST = 32

V7X_VMEM_BYTES = 64 * 1024 * 1024
VMEM_LIMIT_BYTES = 60000 * 1024


def _rms(x, g):
    return x * lax.rsqrt(jnp.mean(x * x, axis=-1, keepdims=True) + EPS) * g


def _silu(x):
    return x * jax.nn.sigmoid(x)


def _dot(a, b):
    return jnp.dot(a, b, preferred_element_type=F32)


def _dot_nt(a, b, precision=None):
    return lax.dot_general(a, b, (((1,), (1,)), ((), ())), preferred_element_type=F32, precision=precision)


def _const_spec(shape):
    nd = len(shape)
    return pl.BlockSpec(shape, lambda *_: (0,) * nd, pipeline_mode=pl.Buffered(1))


def _rope(x, cosf, sina, sinb):
    return x * cosf + pltpu.roll(x, HEAD_DIM - ROT_DIM // 2, axis=1) * sina + pltpu.roll(x, ROT_DIM // 2, axis=1) * sinb


def _even_kernel(x_ref, cos_ref, sina_ref, sinb_ref, pre_g_ref, w_in_ref, pool_w_ref, pool_s_ref, w_out_ref,
                 post_g_ref, o_ref, k_scr, vt_scr, kbar_scr, sel_scr, pcarry_scr, mix_scr):
    i = pl.program_id(1)
    tq = x_ref.shape[0]
    nb = kbar_scr.shape[0]
    A = ATT_WIDTH

    @pl.when(i == 0)
    def _():
        kbar_scr[...] = jnp.zeros_like(kbar_scr)
        pcarry_scr[...] = jnp.zeros_like(pcarry_scr)

    xb = x_ref[...]
    h = _rms(xb, pre_g_ref[...]).astype(BF16)
    q = _dot(h, w_in_ref[:, 0:A])
    k = _dot(h, w_in_ref[:, A:2 * A])
    v = _dot(h, w_in_ref[:, 2 * A:3 * A])
    ga = _dot(h, w_in_ref[:, 3 * A:4 * A])
    pu = _dot(h, w_in_ref[:, 4 * A:4 * A + POOL_WIDTH])
    gb = _dot(h, w_in_ref[:, 4 * A + POOL_WIDTH:4 * A + 2 * POOL_WIDTH])

    cosf, sina, sinb = cos_ref[...], sina_ref[...], sinb_ref[...]
    hs = [slice(hh * HEAD_DIM, (hh + 1) * HEAD_DIM) for hh in range(ATT_HEADS)]
    q_r = [_rope(q[:, s], cosf, sina, sinb) for s in hs]
    k_r = jnp.concatenate([_rope(k[:, s], cosf, sina, sinb) for s in hs], axis=1)
    k_b = k_r.astype(BF16)
    vt_b = v.T.astype(BF16)

    blk = lax.broadcasted_iota(jnp.int32, (nb, 1), 0)
    past = blk < i
    kbar = kbar_scr[...]
    for hh, s in enumerate(hs):
        g = _dot_nt(kbar[:, s], q_r[hh], precision=lax.Precision.HIGHEST)
        g = jnp.where(past, g, -jnp.inf)
        rank = jnp.zeros(g.shape, F32)
        for m in range(nb):
            gm = g[m:m + 1, :]
            beats = (gm > g) | ((gm == g) & (m < blk))
            rank = rank + jnp.where(beats, 1.0, 0.0)
        sel_scr[hh] = jnp.where(past & (rank < MOBA_TOPK), 1.0, 0.0)

    scale = 1.0 / math.sqrt(HEAD_DIM)
    key_i = lax.broadcasted_iota(jnp.int32, (tq, tq), 0)
    qry_i = lax.broadcasted_iota(jnp.int32, (tq, tq), 1)
    causal = key_i <= qry_i
    for hh, s in enumerate(hs):
        qb = q_r[hh].astype(BF16)
        st = jnp.where(causal, _dot_nt(k_b[:, s], qb) * scale, NEG)
        m0 = jnp.max(st, axis=0, keepdims=True)
        p = jnp.exp(st - m0)
        l0 = jnp.sum(p, axis=0, keepdims=True)
        acc0 = _dot(vt_b[s, :], p.astype(BF16))

        def past_block(j, carry, s=s, qb=qb, hh=hh):
            m_run, l_run, acc = carry
            st = _dot_nt(k_scr[j, :, s], qb) * scale
            st = jnp.where(sel_scr[hh, pl.ds(j, 1), :] > 0.5, st, NEG)
            m_new = jnp.maximum(m_run, jnp.max(st, axis=0, keepdims=True))
            alpha = jnp.exp(m_run - m_new)
            p = jnp.exp(st - m_new)
            l_new = alpha * l_run + jnp.sum(p, axis=0, keepdims=True)
            acc = alpha * acc + _dot(vt_scr[j, s, :], p.astype(BF16))
            return m_new, l_new, acc

        _, l_f, acc_f = lax.fori_loop(0, i, past_block, (m0, l0, acc0))
        att = (acc_f * (1.0 / l_f)).T
        mix_scr[:, s] = (att * _silu(ga[:, s])).astype(BF16)

    k_scr[i] = k_b
    vt_scr[i] = vt_b
    kbar_scr[...] = jnp.where(blk == i, jnp.mean(k_r, axis=0, keepdims=True), kbar)

    ext = jnp.concatenate([pcarry_scr[...], pu], axis=0)
    pcarry_scr[...] = pu[tq - POOL_CARRY:, :]
    tpos = i * tq + lax.broadcasted_iota(jnp.int32, (tq, 1), 0)
    for gi, w in enumerate(POOL_WINDOWS):
        cs = slice(gi * POOL_GROUP_W, (gi + 1) * POOL_GROUP_W)
        cum = ext[:, cs]
        d = 1
        while d < w:
            cum = cum + pltpu.roll(cum, d, axis=0)
            d *= 2
        cnt = jnp.minimum(tpos + 1, w).astype(F32)
        mg = cum[POOL_CARRY:, :] / cnt - pu[:, cs]
        yg = _dot(mg.astype(BF16), pool_w_ref[gi]) * pool_s_ref[:, cs] * _silu(gb[:, cs])
        mix_scr[:, A + gi * POOL_GROUP_W:A + (gi + 1) * POOL_GROUP_W] = yg.astype(BF16)

    y = _dot(mix_scr[...], w_out_ref[...])
    o_ref[...] = xb + _rms(y, post_g_ref[...])


def _rope_tables(seq):
    pos = jnp.arange(seq, dtype=F32)
    inv = jnp.power(ROPE_THETA, -jnp.arange(0, ROT_DIM, 2, dtype=F32) / ROT_DIM)
    ang = pos[:, None] * inv[None, :]
    cos, sin = jnp.cos(ang), jnp.sin(ang)
    half = ROT_DIM // 2
    zeros = jnp.zeros((seq, HEAD_DIM - ROT_DIM), F32)
    zh = jnp.zeros((seq, half), F32)
    cosf = jnp.concatenate([cos, cos, jnp.ones((seq, HEAD_DIM - ROT_DIM), F32)], axis=1)
    sina = jnp.concatenate([-sin, zh, zeros], axis=1)
    sinb = jnp.concatenate([zh, sin, zeros], axis=1)
    return cosf, sina, sinb


def _even_layer(x, pre_g, post_g, w_in, pool_w, pool_scale, w_out):
    b, s, d = x.shape
    tq = MOBA_BLOCK
    nb = s // tq
    cosf, sina, sinb = _rope_tables(s)
    tab_spec = pl.BlockSpec((tq, HEAD_DIM), lambda bb, ii: (ii, 0))
    x_spec = pl.BlockSpec((None, tq, d), lambda bb, ii: (bb, ii, 0))
    return pl.pallas_call(
        _even_kernel,
        grid=(b, nb),
        in_specs=[x_spec, tab_spec, tab_spec, tab_spec,
                  _const_spec((1, d)), _const_spec(w_in.shape), _const_spec(pool_w.shape),
                  _const_spec((1, POOL_WIDTH)), _const_spec(w_out.shape), _const_spec((1, d))],
        out_specs=x_spec,
        out_shape=jax.ShapeDtypeStruct(x.shape, x.dtype),
        scratch_shapes=[
            pltpu.VMEM((nb, tq, ATT_WIDTH), BF16),
            pltpu.VMEM((nb, ATT_WIDTH, tq), BF16),
            pltpu.VMEM((nb, ATT_WIDTH), F32),
            pltpu.VMEM((ATT_HEADS, nb, tq), F32),
            pltpu.VMEM((POOL_CARRY, POOL_WIDTH), F32),
            pltpu.VMEM((tq, ATT_WIDTH + POOL_WIDTH), BF16),
        ],
        compiler_params=pltpu.CompilerParams(dimension_semantics=("arbitrary", "arbitrary"),
                                             vmem_limit_bytes=VMEM_LIMIT_BYTES),
        name="even_layer",
    )(x, cosf, sina, sinb, pre_g.reshape(1, d), w_in.astype(BF16), pool_w.astype(BF16),
      pool_scale.reshape(1, POOL_WIDTH), w_out.astype(BF16), post_g.reshape(1, d))


def _odd_kernel(x_ref, pre_g_ref, w_in_ref, bb_ref, cc_ref, lamr_ref, lami_ref, d_ref, glu_w_ref, glu_b_ref,
                dw_ref, ln_g_ref, ln_b_ref, pw_ref, w_out_ref, post_g_ref, o_ref,
                st_re, st_im, bu_scr, xs_scr, gext_scr, mix_scr, *, nbatch, tl):
    step = pl.program_id(0)
    rows = tl * nbatch
    hist = CONV_HIST * nbatch
    W = S5_WIDTH

    @pl.when(step == 0)
    def _():
        st_re[...] = jnp.zeros_like(st_re)
        st_im[...] = jnp.zeros_like(st_im)
        gext_scr[0:hist, :] = jnp.zeros((hist, CONV_WIDTH), F32)

    xb = x_ref[...]
    h = _rms(xb, pre_g_ref[...]).astype(BF16)
    su = _dot(h, w_in_ref[:, 0:W])
    gc = _dot(h, w_in_ref[:, W:2 * W])
    ca = _dot(h, w_in_ref[:, 2 * W:2 * W + CONV_WIDTH])
    cb = _dot(h, w_in_ref[:, 2 * W + CONV_WIDTH:2 * W + 2 * CONV_WIDTH])
    gd = _dot(h, w_in_ref[:, 2 * W + 2 * CONV_WIDTH:2 * W + 3 * CONV_WIDTH])

    su_b = su.astype(BF16)
    hw = 2 * S5_HALF_LANES
    half_in = S5_HALF_GROUPS * S5_GROUP_IN
    for hf in range(2):
        bu_scr[:, hf * hw:(hf + 1) * hw] = _dot(su_b[:, hf * half_in:(hf + 1) * half_in], bb_ref[hf])

    for c in range(2 * S5_HALF_LANES // S5_SCAN_LANES):
        hf, sub = divmod(c, S5_HALF_LANES // S5_SCAN_LANES)
        cl = slice(c * S5_SCAN_LANES, (c + 1) * S5_SCAN_LANES)
        re0 = hf * hw + sub * S5_SCAN_LANES
        im0 = re0 + S5_HALF_LANES
        ar, ai = lamr_ref[:, cl], lami_ref[:, cl]

        def token(t, carry, re0=re0, im0=im0, ar=ar, ai=ai):
            sr, si = carry
            r0 = pl.multiple_of(t * nbatch, nbatch)
            br = bu_scr[pl.ds(r0, nbatch), re0:re0 + S5_SCAN_LANES]
            bi = bu_scr[pl.ds(r0, nbatch), im0:im0 + S5_SCAN_LANES]
            nr = ar * sr - ai * si + br
            ni = ar * si + ai * sr + bi
            xs_scr[pl.ds(r0, nbatch), re0:re0 + S5_SCAN_LANES] = nr.astype(BF16)
            xs_scr[pl.ds(r0, nbatch), im0:im0 + S5_SCAN_LANES] = ni.astype(BF16)
            return nr, ni

        sr, si = lax.fori_loop(0, tl, token, (st_re[:, cl], st_im[:, cl]))
        st_re[:, cl] = sr
        st_im[:, cl] = si

    y = jnp.concatenate([_dot(xs_scr[:, hf * hw:(hf + 1) * hw], cc_ref[hf]) for hf in range(2)], axis=1)
    y = y + d_ref[...] * su
    yy = _dot(y.astype(BF16), glu_w_ref[...]) + glu_b_ref[...]
    mix_scr[:, 0:W] = (yy[:, 0:W] * jax.nn.sigmoid(yy[:, W:2 * W]) * _silu(gc)).astype(BF16)

    gext_scr[hist:hist + rows, :] = ca * jax.nn.sigmoid(cb)
    lead = CONV_HIST - (CONV_K - 1)
    conv = jnp.zeros((rows, CONV_WIDTH), F32)
    for kk in range(CONV_K):
        r0 = (lead + kk) * nbatch
        conv = conv + dw_ref[kk:kk + 1, :] * gext_scr[r0:r0 + rows, :]
    gext_scr[0:hist, :] = gext_scr[rows:rows + hist, :]
    mu = jnp.mean(conv, axis=-1, keepdims=True)
    cen = conv - mu
    var = jnp.mean(cen * cen, axis=-1, keepdims=True)
    nrm = cen * lax.rsqrt(var + EPS) * ln_g_ref[...] + ln_b_ref[...]
    mix_scr[:, W:W + CONV_WIDTH] = (_dot(_silu(nrm).astype(BF16), pw_ref[...]) * _silu(gd)).astype(BF16)

    yo = _dot(mix_scr[...], w_out_ref[...])
    o_ref[...] = xb + _rms(yo, post_g_ref[...])


def _s5_operands(lam_re, lam_im, log_dt, b_re, b_im, c_re, c_im, nbatch):
    lr, li = lam_re.astype(F32), lam_im.astype(F32)
    dt = jnp.exp(log_dt.astype(F32))[:, None]
    mag = jnp.exp(lr * dt)
    ab_re = mag * jnp.cos(li * dt)
    ab_im = mag * jnp.sin(li * dt)
    den = lr * lr + li * li
    num_re = ab_re - 1.0
    f_re = (num_re * lr + ab_im * li) / den
    f_im = (ab_im * lr - num_re * li) / den
    br, bi = b_re.astype(F32), b_im.astype(F32)
    bb_re = f_re[..., None] * br - f_im[..., None] * bi
    bb_im = f_re[..., None] * bi + f_im[..., None] * br
    hg, p, n_in = S5_HALF_GROUPS, S5_STATE, S5_GROUP_IN
    eye = jnp.eye(hg, dtype=F32)

    def pack_b(m):
        return jnp.einsum('hgpj,gk->hgjkp', m.reshape(2, hg, p, n_in), eye).reshape(2, hg * n_in, hg * p)

    def pack_c(m):
        return jnp.einsum('hgip,gk->hgpki', m.reshape(2, hg, n_in, p), eye).reshape(2, hg * p, hg * n_in)

    bb = jnp.concatenate([pack_b(bb_re), pack_b(bb_im)], axis=2).astype(BF16)
    cc = jnp.concatenate([pack_c(c_re.astype(F32)), -pack_c(c_im.astype(F32))], axis=1).astype(BF16)
    lanes = S5_GROUPS * S5_STATE
    lamr = jnp.broadcast_to(ab_re.reshape(1, lanes), (nbatch, lanes))
    lami = jnp.broadcast_to(ab_im.reshape(1, lanes), (nbatch, lanes))
    return bb, cc, lamr, lami


def _odd_layer(xt, nbatch, pre_g, post_g, w_in, lam_re, lam_im, log_dt, b_re, b_im, c_re, c_im, d_skip,
               glu_w, glu_b, dw, ln_g, ln_b, pw, w_out, tl=32):
    n, d = xt.shape
    rows = tl * nbatch
    bb, cc, lamr, lami = _s5_operands(lam_re, lam_im, log_dt, b_re, b_im, c_re, c_im, nbatch)
    lanes = S5_GROUPS * S5_STATE
    x_spec = pl.BlockSpec((rows, d), lambda ss: (ss, 0))
    ops = (xt, pre_g.reshape(1, d), w_in.astype(BF16), bb, cc, lamr, lami, d_skip.reshape(1, S5_WIDTH),
           glu_w.astype(BF16), glu_b.reshape(1, 2 * S5_WIDTH), dw.astype(F32), ln_g.reshape(1, CONV_WIDTH),
           ln_b.reshape(1, CONV_WIDTH), pw.astype(BF16), w_out.astype(BF16), post_g.reshape(1, d))
    return pl.pallas_call(
        functools.partial(_odd_kernel, nbatch=nbatch, tl=tl),
        grid=(n // rows,),
        in_specs=[x_spec] + [_const_spec(o.shape) for o in ops[1:]],
        out_specs=x_spec,
        out_shape=jax.ShapeDtypeStruct(xt.shape, xt.dtype),
        scratch_shapes=[
            pltpu.VMEM((nbatch, lanes), F32),
            pltpu.VMEM((nbatch, lanes), F32),
            pltpu.VMEM((rows, 2 * lanes), F32),
            pltpu.VMEM((rows, 2 * lanes), BF16),
            pltpu.VMEM((CONV_HIST * nbatch + rows, CONV_WIDTH), F32),
            pltpu.VMEM((rows, S5_WIDTH + CONV_WIDTH), BF16),
        ],
        compiler_params=pltpu.CompilerParams(dimension_semantics=("arbitrary",),
                                             vmem_limit_bytes=VMEM_LIMIT_BYTES),
        name="odd_layer",
    )(*ops)


def kernel(x, pre_norm_g, post_norm_g, e_w_in, e_pool_w, e_pool_scale, e_w_out, o_w_in, o_lam_re, o_lam_im,
           o_log_dt, o_b_re, o_b_im, o_c_re, o_c_im, o_d, o_glu_w, o_glu_b, o_dw, o_ln_g, o_ln_b, o_pw, o_w_out):
    b, s, d = x.shape
    for i in range(pre_norm_g.shape[0]):
        j = i // 2
        if i % 2 == 0:
            x = _even_layer(x, pre_norm_g[i], post_norm_g[i], e_w_in[j], e_pool_w[j], e_pool_scale[j], e_w_out[j])
        else:
            xt = jnp.transpose(x, (1, 0, 2)).reshape(s * b, d)
            xt = _odd_layer(xt, b, pre_norm_g[i], post_norm_g[i], o_w_in[j], o_lam_re[j], o_lam_im[j],
                            o_log_dt[j], o_b_re[j], o_b_im[j], o_c_re[j], o_c_im[j], o_d[j], o_glu_w[j],
                            o_glu_b[j], o_dw[j], o_ln_g[j], o_ln_b[j], o_pw[j], o_w_out[j])
            x = jnp.transpose(xt.reshape(s, b, d), (1, 0, 2))
    return x
```

```python
import functools
import math

import jax
import jax.numpy as jnp
from jax import lax
from jax.experimental import pallas as pl
from jax.experimental.pallas import tpu as pltpu

F32 = jnp.float32
BF16 = jnp.bfloat16

D_MODEL = 1024
EPS = 1e-6
NEG = -1e30

ATT_WIDTH = 512
HEAD_DIM = 128
ATT_HEADS = 4
ROT_DIM = 32
ROPE_THETA = 500000.0
MOBA_BLOCK = 256
MOBA_TOPK = 3
POOL_WIDTH = 512
POOL_WINDOWS = (2, 4, 8, 16)
POOL_GROUP_W = 128
POOL_CARRY = 16

S5_WIDTH = 512
S5_GROUP_IN = 16
S5_GROUPS = 32
S5_STATE = 64
S5_HALF_GROUPS = 16
S5_HALF_LANES = S5_HALF_GROUPS * S5_STATE
S5_SCAN_LANES = 512
CONV_WIDTH = 512
CONV_K = 31
CONV_HIST = 32
CONV_GROUP_TOKENS = 8
CONV_TAP_BLOCK = 8

SUBLANES = 8
LANES = 128
MXU_COLS = 256

V7X_VMEM_BYTES = 64 * 1024 * 1024
VMEM_LIMIT_BYTES = 60000 * 1024


def _rms(x, g):
    return x * lax.rsqrt(jnp.mean(x * x, axis=-1, keepdims=True) + EPS) * g


def _silu(x):
    return x * jax.nn.sigmoid(x)


def _dot(a, b):
    return jnp.dot(a, b, preferred_element_type=F32)


def _dot_nt(a, b, precision=None):
    return lax.dot_general(a, b, (((1,), (1,)), ((), ())), preferred_element_type=F32, precision=precision)


def _const_spec(shape):
    nd = len(shape)
    return pl.BlockSpec(shape, lambda *_: (0,) * nd, pipeline_mode=pl.Buffered(1))


def _rope(x, cosf, sina, sinb):
    return x * cosf + pltpu.roll(x, HEAD_DIM - ROT_DIM // 2, axis=1) * sina + pltpu.roll(x, ROT_DIM // 2, axis=1) * sinb


def _even_kernel(x_ref, cos_ref, sina_ref, sinb_ref, pre_g_ref, w_in_ref, pool_w_ref, pool_s_ref, w_out_ref,
                 post_g_ref, o_ref, k_scr, vt_scr, kbar_scr, sel_scr, acc_scr, pcarry_scr, mix_scr):
    i = pl.program_id(1)
    tq = x_ref.shape[0]
    nb = kbar_scr.shape[0]
    A = ATT_WIDTH

    @pl.when(i == 0)
    def _():
        kbar_scr[...] = jnp.zeros_like(kbar_scr)
        pcarry_scr[...] = jnp.zeros_like(pcarry_scr)

    xb = x_ref[...]
    h = _rms(xb, pre_g_ref[...]).astype(BF16)
    q = _dot(h, w_in_ref[:, 0:A])
    k = _dot(h, w_in_ref[:, A:2 * A])
    v = _dot(h, w_in_ref[:, 2 * A:3 * A])
    ga = _dot(h, w_in_ref[:, 3 * A:4 * A])
    pu = _dot(h, w_in_ref[:, 4 * A:4 * A + POOL_WIDTH])
    gb = _dot(h, w_in_ref[:, 4 * A + POOL_WIDTH:4 * A + 2 * POOL_WIDTH])

    cosf, sina, sinb = cos_ref[...], sina_ref[...], sinb_ref[...]
    hs = [slice(hh * HEAD_DIM, (hh + 1) * HEAD_DIM) for hh in range(ATT_HEADS)]
    q_r = [_rope(q[:, s], cosf, sina, sinb) for s in hs]
    k_r = jnp.concatenate([_rope(k[:, s], cosf, sina, sinb) for s in hs], axis=1)
    k_b = k_r.astype(BF16)
    vt_b = v.T.astype(BF16)

    blk = lax.broadcasted_iota(jnp.int32, (nb, 1), 0)
    past = blk < i
    kbar = kbar_scr[...]
    for hh, s in enumerate(hs):
        g = _dot_nt(kbar[:, s], q_r[hh], precision=lax.Precision.HIGHEST)
        g = jnp.where(past, g, -jnp.inf)
        rank = jnp.zeros(g.shape, F32)
        for m in range(nb):
            gm = g[m:m + 1, :]
            beats = (gm > g) | ((gm == g) & (m < blk))
            rank = rank + jnp.where(beats, 1.0, 0.0)
        sel_scr[hh] = jnp.where(past & (rank < MOBA_TOPK), 1.0, 0.0)

    qscale = math.log2(math.e) / math.sqrt(HEAD_DIM)
    qb = [(qh * qscale).astype(BF16) for qh in q_r]
    key_i = lax.broadcasted_iota(jnp.int32, (tq, tq), 0)
    qry_i = lax.broadcasted_iota(jnp.int32, (tq, tq), 1)
    causal = key_i <= qry_i
    nh = range(ATT_HEADS)
    st0 = [jnp.where(causal, _dot_nt(k_b[:, hs[hh]], qb[hh]), NEG) for hh in nh]
    m0 = [jnp.max(st0[hh], axis=0, keepdims=True) for hh in nh]
    p0 = [jnp.exp2(st0[hh] - m0[hh]) for hh in nh]
    l0 = [jnp.sum(p0[hh], axis=0, keepdims=True) for hh in nh]
    for hh in nh:
        acc_scr[hh] = _dot(vt_b[hs[hh], :], p0[hh].astype(BF16))

    def past_block(j, carry):
        m_run, l_run = carry
        st = [_dot_nt(k_scr[j, :, hs[hh]], qb[hh]) for hh in nh]
        st = [jnp.where(sel_scr[hh, pl.ds(j, 1), :] > 0.5, st[hh], NEG) for hh in nh]
        m_new = [jnp.maximum(m_run[hh], jnp.max(st[hh], axis=0, keepdims=True)) for hh in nh]
        alpha = [jnp.exp2(m_run[hh] - m_new[hh]) for hh in nh]
        p = [jnp.exp2(st[hh] - m_new[hh]) for hh in nh]
        l_new = [alpha[hh] * l_run[hh] + jnp.sum(p[hh], axis=0, keepdims=True) for hh in nh]
        pv = [_dot(vt_scr[j, hs[hh], :], p[hh].astype(BF16)) for hh in nh]
        for hh in nh:
            acc_scr[hh] = alpha[hh] * acc_scr[hh] + pv[hh]
        return tuple(m_new), tuple(l_new)

    _, l_f = lax.fori_loop(0, i, past_block, (tuple(m0), tuple(l0)))
    for hh, s in enumerate(hs):
        att = (acc_scr[hh] * (1.0 / l_f[hh])).T
        mix_scr[:, s] = (att * _silu(ga[:, s])).astype(BF16)

    k_scr[i] = k_b
    vt_scr[i] = vt_b
    kbar_scr[...] = jnp.where(blk == i, jnp.mean(k_r, axis=0, keepdims=True), kbar)

    ext = jnp.concatenate([pcarry_scr[...], pu], axis=0)
    pcarry_scr[...] = pu[tq - POOL_CARRY:, :]
    tpos = i * tq + lax.broadcasted_iota(jnp.int32, (tq, 1), 0)
    for gi, w in enumerate(POOL_WINDOWS):
        cs = slice(gi * POOL_GROUP_W, (gi + 1) * POOL_GROUP_W)
        cum = ext[:, cs]
        d = 1
        while d < w:
            cum = cum + pltpu.roll(cum, d, axis=0)
            d *= 2
        cnt = jnp.minimum(tpos + 1, w).astype(F32)
        mg = cum[POOL_CARRY:, :] / cnt - pu[:, cs]
        yg = _dot(mg.astype(BF16), pool_w_ref[gi]) * pool_s_ref[:, cs] * _silu(gb[:, cs])
        mix_scr[:, A + gi * POOL_GROUP_W:A + (gi + 1) * POOL_GROUP_W] = yg.astype(BF16)

    y = _dot(mix_scr[...], w_out_ref[...])
    o_ref[...] = xb + _rms(y, post_g_ref[...])


def _rope_tables(seq):
    pos = jnp.arange(seq, dtype=F32)
    inv = jnp.power(ROPE_THETA, -jnp.arange(0, ROT_DIM, 2, dtype=F32) / ROT_DIM)
    ang = pos[:, None] * inv[None, :]
    cos, sin = jnp.cos(ang), jnp.sin(ang)
    half = ROT_DIM // 2
    zeros = jnp.zeros((seq, HEAD_DIM - ROT_DIM), F32)
    zh = jnp.zeros((seq, half), F32)
    cosf = jnp.concatenate([cos, cos, jnp.ones((seq, HEAD_DIM - ROT_DIM), F32)], axis=1)
    sina = jnp.concatenate([-sin, zh, zeros], axis=1)
    sinb = jnp.concatenate([zh, sin, zeros], axis=1)
    return cosf, sina, sinb


def _even_layer(x, pre_g, post_g, w_in, pool_w, pool_scale, w_out):
    b, s, d = x.shape
    tq = MOBA_BLOCK
    nb = s // tq
    cosf, sina, sinb = _rope_tables(s)
    tab_spec = pl.BlockSpec((tq, HEAD_DIM), lambda bb, ii: (ii, 0))
    x_spec = pl.BlockSpec((None, tq, d), lambda bb, ii: (bb, ii, 0))
    return pl.pallas_call(
        _even_kernel,
        grid=(b, nb),
        in_specs=[x_spec, tab_spec, tab_spec, tab_spec,
                  _const_spec((1, d)), _const_spec(w_in.shape), _const_spec(pool_w.shape),
                  _const_spec((1, POOL_WIDTH)), _const_spec(w_out.shape), _const_spec((1, d))],
        out_specs=x_spec,
        out_shape=jax.ShapeDtypeStruct(x.shape, x.dtype),
        scratch_shapes=[
            pltpu.VMEM((nb, tq, ATT_WIDTH), BF16),
            pltpu.VMEM((nb, ATT_WIDTH, tq), BF16),
            pltpu.VMEM((nb, ATT_WIDTH), F32),
            pltpu.VMEM((ATT_HEADS, nb, tq), F32),
            pltpu.VMEM((ATT_HEADS, HEAD_DIM, tq), F32),
            pltpu.VMEM((POOL_CARRY, POOL_WIDTH), F32),
            pltpu.VMEM((tq, ATT_WIDTH + POOL_WIDTH), BF16),
        ],
        compiler_params=pltpu.CompilerParams(dimension_semantics=("arbitrary", "arbitrary"),
                                             vmem_limit_bytes=VMEM_LIMIT_BYTES),
        name="even_layer",
    )(x, cosf, sina, sinb, pre_g.reshape(1, d), w_in.astype(BF16), pool_w.astype(BF16),
      pool_scale.reshape(1, POOL_WIDTH), w_out.astype(BF16), post_g.reshape(1, d))


def _odd_kernel(x_ref, pre_g_ref, w_in_ref, bb_ref, cc_ref, lamr_ref, lami_ref, d_ref, glu_w_ref, glu_b_ref,
                dwb_ref, ln_g_ref, ln_b_ref, pw_ref, w_out_ref, post_g_ref, o_ref,
                st_re, st_im, h_scr, su_scr, gc_scr, gd_scr, bu_scr, xs_scr, gext_scr, conv_scr, mix_scr,
                *, nbatch, tl):
    step = pl.program_id(0)
    rows = tl * nbatch
    hist = CONV_HIST * nbatch
    W = S5_WIDTH

    @pl.when(step == 0)
    def _():
        st_re[...] = jnp.zeros_like(st_re)
        st_im[...] = jnp.zeros_like(st_im)
        gext_scr[0:hist, :] = jnp.zeros((hist, CONV_WIDTH), F32)

    xb = x_ref[...]
    h_scr[...] = _rms(xb, pre_g_ref[...]).astype(BF16)
    hw = 2 * S5_HALF_LANES
    half_in = S5_HALF_GROUPS * S5_GROUP_IN
    mc = MXU_COLS

    def project(dst, col0, c):
        def run():
            dst[:, c * mc:(c + 1) * mc] = _dot(h_scr[...], w_in_ref[:, col0 + c * mc:col0 + (c + 1) * mc])
        return run

    def s5_in(hf, c):
        def run():
            su_b = su_scr[:, hf * half_in:(hf + 1) * half_in].astype(BF16)
            bu_scr[:, hf * hw + c * 2 * mc:hf * hw + (c + 1) * 2 * mc] = _dot(su_b, bb_ref[hf, :, c * 2 * mc:(c + 1) * 2 * mc])
        return run

    mxu_work = [project(su_scr, 0, c) for c in range(W // mc)]
    mxu_work += [s5_in(hf, c) for hf in range(2) for c in range(hw // (2 * mc))]
    mxu_work += [project(gc_scr, W, c) for c in range(W // mc)]
    mxu_work += [project(gd_scr, 2 * W + 2 * CONV_WIDTH, c) for c in range(CONV_WIDTH // mc)]

    ca = _dot(h_scr[...], w_in_ref[:, 2 * W:2 * W + CONV_WIDTH])
    cb = _dot(h_scr[...], w_in_ref[:, 2 * W + CONV_WIDTH:2 * W + 2 * CONV_WIDTH])
    gext_scr[hist:hist + rows, :] = ca * jax.nn.sigmoid(cb)
    lead = CONV_HIST - (CONV_K - 1)
    tiles_per_tok = nbatch // SUBLANES
    ng = CONV_GROUP_TOKENS
    for lt in range(CONV_WIDTH // LANES):
        ls = slice(lt * LANES, (lt + 1) * LANES)
        for par in range(tiles_per_tok):
            for t0 in range(0, tl, ng):
                acc = [jnp.zeros((SUBLANES, LANES), F32) for _ in range(ng)]
                for k0 in range(0, CONV_K, CONV_TAP_BLOCK):
                    kn = min(CONV_TAP_BLOCK, CONV_K - k0)
                    taps = [dwb_ref[k0 + kk, :, ls] for kk in range(kn)]
                    for u in range(ng + kn - 1):
                        r0 = ((t0 + lead + k0 + u) * tiles_per_tok + par) * SUBLANES
                        xin = gext_scr[r0:r0 + SUBLANES, ls]
                        for m in range(max(0, u - kn + 1), min(ng, u + 1)):
                            acc[m] = acc[m] + taps[u - m] * xin
                for m in range(ng):
                    r0 = ((t0 + m) * tiles_per_tok + par) * SUBLANES
                    conv_scr[r0:r0 + SUBLANES, ls] = acc[m]
                if mxu_work:
                    mxu_work.pop(0)()
    while mxu_work:
        mxu_work.pop(0)()
    gext_scr[0:hist, :] = gext_scr[rows:rows + hist, :]
    conv = conv_scr[...]
    mu = jnp.mean(conv, axis=-1, keepdims=True)
    cen = conv - mu
    var = jnp.mean(cen * cen, axis=-1, keepdims=True)
    nrm = cen * lax.rsqrt(var + EPS) * ln_g_ref[...] + ln_b_ref[...]
    mix_scr[:, W:W + CONV_WIDTH] = (_dot(_silu(nrm).astype(BF16), pw_ref[...]) * _silu(gd_scr[...])).astype(BF16)

    y_half = []
    for c in range(2 * S5_HALF_LANES // S5_SCAN_LANES):
        hf, sub = divmod(c, S5_HALF_LANES // S5_SCAN_LANES)
        cl = slice(c * S5_SCAN_LANES, (c + 1) * S5_SCAN_LANES)
        re = slice(hf * hw + sub * S5_SCAN_LANES, hf * hw + (sub + 1) * S5_SCAN_LANES)
        im = slice(re.start + S5_HALF_LANES, re.stop + S5_HALF_LANES)
        ar, ai = lamr_ref[:, cl], lami_ref[:, cl]
        sr, si = st_re[:, cl], st_im[:, cl]
        for t in range(tl):
            rs = slice(t * nbatch, (t + 1) * nbatch)
            sr, si = ar * sr - ai * si + bu_scr[rs, re], ar * si + ai * sr + bu_scr[rs, im]
            xs_scr[rs, re] = sr.astype(BF16)
            xs_scr[rs, im] = si.astype(BF16)
        st_re[:, cl] = sr
        st_im[:, cl] = si
        if sub == S5_HALF_LANES // S5_SCAN_LANES - 1:
            y_half.append(_dot(xs_scr[:, hf * hw:(hf + 1) * hw], cc_ref[hf]))

    y = jnp.concatenate(y_half, axis=1) + d_ref[...] * su_scr[...]
    yy = _dot(y.astype(BF16), glu_w_ref[...]) + glu_b_ref[...]
    mix_scr[:, 0:W] = (yy[:, 0:W] * jax.nn.sigmoid(yy[:, W:2 * W]) * _silu(gc_scr[...])).astype(BF16)

    yo = _dot(mix_scr[...], w_out_ref[...])
    o_ref[...] = xb + _rms(yo, post_g_ref[...])


def _s5_operands(lam_re, lam_im, log_dt, b_re, b_im, c_re, c_im, nbatch):
    lr, li = lam_re.astype(F32), lam_im.astype(F32)
    dt = jnp.exp(log_dt.astype(F32))[:, None]
    mag = jnp.exp(lr * dt)
    ab_re = mag * jnp.cos(li * dt)
    ab_im = mag * jnp.sin(li * dt)
    den = lr * lr + li * li
    num_re = ab_re - 1.0
    f_re = (num_re * lr + ab_im * li) / den
    f_im = (ab_im * lr - num_re * li) / den
    br, bi = b_re.astype(F32), b_im.astype(F32)
    bb_re = f_re[..., None] * br - f_im[..., None] * bi
    bb_im = f_re[..., None] * bi + f_im[..., None] * br
    hg, p, n_in = S5_HALF_GROUPS, S5_STATE, S5_GROUP_IN
    eye = jnp.eye(hg, dtype=F32)

    def pack_b(m):
        return jnp.einsum('hgpj,gk->hgjkp', m.reshape(2, hg, p, n_in), eye).reshape(2, hg * n_in, hg * p)

    def pack_c(m):
        return jnp.einsum('hgip,gk->hgpki', m.reshape(2, hg, n_in, p), eye).reshape(2, hg * p, hg * n_in)

    bb = jnp.concatenate([pack_b(bb_re), pack_b(bb_im)], axis=2).astype(BF16)
    cc = jnp.concatenate([pack_c(c_re.astype(F32)), -pack_c(c_im.astype(F32))], axis=1).astype(BF16)
    lanes = S5_GROUPS * S5_STATE
    lamr = jnp.broadcast_to(ab_re.reshape(1, lanes), (nbatch, lanes))
    lami = jnp.broadcast_to(ab_im.reshape(1, lanes), (nbatch, lanes))
    return bb, cc, lamr, lami


def _odd_layer(xt, nbatch, pre_g, post_g, w_in, lam_re, lam_im, log_dt, b_re, b_im, c_re, c_im, d_skip,
               glu_w, glu_b, dw, ln_g, ln_b, pw, w_out, tl=32):
    n, d = xt.shape
    rows = tl * nbatch
    bb, cc, lamr, lami = _s5_operands(lam_re, lam_im, log_dt, b_re, b_im, c_re, c_im, nbatch)
    lanes = S5_GROUPS * S5_STATE
    x_spec = pl.BlockSpec((rows, d), lambda ss: (ss, 0))
    ops = (xt, pre_g.reshape(1, d), w_in.astype(BF16), bb, cc, lamr, lami, d_skip.reshape(1, S5_WIDTH),
           glu_w.astype(BF16), glu_b.reshape(1, 2 * S5_WIDTH),
           jnp.broadcast_to(dw.astype(F32)[:, None, :], (CONV_K, SUBLANES, CONV_WIDTH)), ln_g.reshape(1, CONV_WIDTH),
           ln_b.reshape(1, CONV_WIDTH), pw.astype(BF16), w_out.astype(BF16), post_g.reshape(1, d))
    return pl.pallas_call(
        functools.partial(_odd_kernel, nbatch=nbatch, tl=tl),
        grid=(n // rows,),
        in_specs=[x_spec] + [_const_spec(o.shape) for o in ops[1:]],
        out_specs=x_spec,
        out_shape=jax.ShapeDtypeStruct(xt.shape, xt.dtype),
        scratch_shapes=[
            pltpu.VMEM((nbatch, lanes), F32),
            pltpu.VMEM((nbatch, lanes), F32),
            pltpu.VMEM((rows, d), BF16),
            pltpu.VMEM((rows, S5_WIDTH), F32),
            pltpu.VMEM((rows, S5_WIDTH), F32),
            pltpu.VMEM((rows, CONV_WIDTH), F32),
            pltpu.VMEM((rows, 2 * lanes), F32),
            pltpu.VMEM((rows, 2 * lanes), BF16),
            pltpu.VMEM((CONV_HIST * nbatch + rows, CONV_WIDTH), F32),
            pltpu.VMEM((rows, CONV_WIDTH), F32),
            pltpu.VMEM((rows, S5_WIDTH + CONV_WIDTH), BF16),
        ],
        compiler_params=pltpu.CompilerParams(dimension_semantics=("arbitrary",),
                                             vmem_limit_bytes=VMEM_LIMIT_BYTES),
        name="odd_layer",
    )(*ops)


def kernel(x, pre_norm_g, post_norm_g, e_w_in, e_pool_w, e_pool_scale, e_w_out, o_w_in, o_lam_re, o_lam_im,
           o_log_dt, o_b_re, o_b_im, o_c_re, o_c_im, o_d, o_glu_w, o_glu_b, o_dw, o_ln_g, o_ln_b, o_pw, o_w_out):
    b, s, d = x.shape
    for i in range(pre_norm_g.shape[0]):
        j = i // 2
        if i % 2 == 0:
            x = _even_layer(x, pre_norm_g[i], post_norm_g[i], e_w_in[j], e_pool_w[j], e_pool_scale[j], e_w_out[j])
        else:
            xt = jnp.transpose(x, (1, 0, 2)).reshape(s * b, d)
            xt = _odd_layer(xt, b, pre_norm_g[i], post_norm_g[i], o_w_in[j], o_lam_re[j], o_lam_im[j],
                            o_log_dt[j], o_b_re[j], o_b_im[j], o_c_re[j], o_c_im[j], o_d[j], o_glu_w[j],
                            o_glu_b[j], o_dw[j], o_ln_g[j], o_ln_b[j], o_pw[j], o_w_out[j])
            x = jnp.transpose(xt.reshape(s, b, d), (1, 0, 2))
    return x
```

```python
import functools
import math

import jax
import jax.numpy as jnp
from jax import lax
from jax.experimental import pallas as pl
from jax.experimental.pallas import tpu as pltpu

F32 = jnp.float32
BF16 = jnp.bfloat16

D_MODEL = 1024
EPS = 1e-6
NEG = -1e30

ATT_WIDTH = 512
HEAD_DIM = 128
ATT_HEADS = 4
ROT_DIM = 32
ROPE_THETA = 500000.0
MOBA_BLOCK = 256
MOBA_TOPK = 3
POOL_WIDTH = 512
POOL_WINDOWS = (2, 4, 8, 16)
POOL_GROUP_W = 128
POOL_CARRY = 16

S5_WIDTH = 512
S5_GROUP_IN = 16
S5_GROUPS = 32
S5_STATE = 64
S5_HALF_GROUPS = 16
S5_HALF_LANES = S5_HALF_GROUPS * S5_STATE
S5_SCAN_LANES = 512
CONV_WIDTH = 512
CONV_K = 31
CONV_HIST = 32
CONV_GROUP_TOKENS = 8
CONV_TAP_BLOCK = 8
CONV_MXU_SLACK = 5

SUBLANES = 8
LANES = 128
MXU_COLS = 256

V7X_VMEM_BYTES = 64 * 1024 * 1024
VMEM_LIMIT_BYTES = 60000 * 1024


def _rms(x, g):
    return x * lax.rsqrt(jnp.mean(x * x, axis=-1, keepdims=True) + EPS) * g


def _silu(x):
    return x * jax.nn.sigmoid(x)


def _dot(a, b):
    return jnp.dot(a, b, preferred_element_type=F32)


def _dot_nt(a, b, precision=None):
    return lax.dot_general(a, b, (((1,), (1,)), ((), ())), preferred_element_type=F32, precision=precision)


def _const_spec(shape):
    nd = len(shape)
    return pl.BlockSpec(shape, lambda *_: (0,) * nd, pipeline_mode=pl.Buffered(1))


def _rope(x, cosf, sina, sinb):
    return x * cosf + pltpu.roll(x, HEAD_DIM - ROT_DIM // 2, axis=1) * sina + pltpu.roll(x, ROT_DIM // 2, axis=1) * sinb


def _even_kernel(x_ref, cos_ref, sina_ref, sinb_ref, pre_g_ref, w_in_ref, pool_w_ref, pool_s_ref, w_out_ref,
                 post_g_ref, o_ref, k_scr, vt_scr, kbar_scr, sel_scr, acc_scr, pcarry_scr, mix_scr):
    i = pl.program_id(1)
    tq = x_ref.shape[0]
    nb = kbar_scr.shape[0]
    A = ATT_WIDTH

    @pl.when(i == 0)
    def _():
        kbar_scr[...] = jnp.zeros_like(kbar_scr)
        pcarry_scr[...] = jnp.zeros_like(pcarry_scr)

    xb = x_ref[...]
    h = _rms(xb, pre_g_ref[...]).astype(BF16)
    q = _dot(h, w_in_ref[:, 0:A])
    k = _dot(h, w_in_ref[:, A:2 * A])
    v = _dot(h, w_in_ref[:, 2 * A:3 * A])
    ga = _dot(h, w_in_ref[:, 3 * A:4 * A])
    pu = _dot(h, w_in_ref[:, 4 * A:4 * A + POOL_WIDTH])
    gb = _dot(h, w_in_ref[:, 4 * A + POOL_WIDTH:4 * A + 2 * POOL_WIDTH])

    cosf, sina, sinb = cos_ref[...], sina_ref[...], sinb_ref[...]
    hs = [slice(hh * HEAD_DIM, (hh + 1) * HEAD_DIM) for hh in range(ATT_HEADS)]
    q_r = [_rope(q[:, s], cosf, sina, sinb) for s in hs]
    k_r = jnp.concatenate([_rope(k[:, s], cosf, sina, sinb) for s in hs], axis=1)
    k_b = k_r.astype(BF16)
    vt_b = v.T.astype(BF16)

    blk = lax.broadcasted_iota(jnp.int32, (nb, 1), 0)
    past = blk < i
    kbar = kbar_scr[...]
    sel = []
    for hh, s in enumerate(hs):
        g = _dot_nt(kbar[:, s], q_r[hh], precision=lax.Precision.HIGHEST)
        g = jnp.where(past, g, -jnp.inf)
        rank = jnp.zeros(g.shape, F32)
        for m in range(nb):
            gm = g[m:m + 1, :]
            beats = (gm > g) | ((gm == g) & (m < blk))
            rank = rank + jnp.where(beats, 1.0, 0.0)
        sel.append(jnp.where(past & (rank < MOBA_TOPK), 1.0, 0.0))

    k_scr[i] = k_b
    vt_scr[i] = vt_b
    kbar_scr[...] = jnp.where(blk == i, jnp.mean(k_r, axis=0, keepdims=True), kbar)

    ext = jnp.concatenate([pcarry_scr[...], pu], axis=0)
    pcarry_scr[...] = pu[tq - POOL_CARRY:, :]
    tpos = i * tq + lax.broadcasted_iota(jnp.int32, (tq, 1), 0)
    for gi, w in enumerate(POOL_WINDOWS):
        cs = slice(gi * POOL_GROUP_W, (gi + 1) * POOL_GROUP_W)
        cum = ext[:, cs]
        d = 1
        while d < w:
            cum = cum + pltpu.roll(cum, d, axis=0)
            d *= 2
        cnt = jnp.minimum(tpos + 1, w).astype(F32)
        mg = cum[POOL_CARRY:, :] / cnt - pu[:, cs]
        yg = _dot(mg.astype(BF16), pool_w_ref[gi]) * pool_s_ref[:, cs] * _silu(gb[:, cs])
        mix_scr[:, A + gi * POOL_GROUP_W:A + (gi + 1) * POOL_GROUP_W] = yg.astype(BF16)

    qscale = math.log2(math.e) / math.sqrt(HEAD_DIM)
    qb = [(qh * qscale).astype(BF16) for qh in q_r]
    gate_a = [_silu(ga[:, s]) for s in hs]
    key_i = lax.broadcasted_iota(jnp.int32, (tq, tq), 0)
    qry_i = lax.broadcasted_iota(jnp.int32, (tq, tq), 1)
    causal = key_i <= qry_i
    nh = range(ATT_HEADS)

    for hh in nh:
        sel_scr[hh] = sel[hh]

    st0 = [jnp.where(causal, _dot_nt(k_b[:, hs[hh]], qb[hh]), NEG) for hh in nh]
    m0 = [jnp.max(st0[hh], axis=0, keepdims=True) for hh in nh]
    p0 = [jnp.exp2(st0[hh] - m0[hh]) for hh in nh]
    l0 = [jnp.sum(p0[hh], axis=0, keepdims=True) for hh in nh]
    for hh in nh:
        acc_scr[hh] = _dot(vt_b[hs[hh], :], p0[hh].astype(BF16))

    def past_pair(jp, carry):
        m_run, l_run = carry
        j0 = 2 * jp
        ch = [(hh, j0 + d) for d in range(2) for hh in nh]
        st = [_dot_nt(k_scr[j, :, hs[hh]], qb[hh]) for hh, j in ch]
        st = [jnp.where(sel_scr[hh, pl.ds(j, 1), :] > 0.5, st[c], NEG) for c, (hh, j) in enumerate(ch)]
        cmax = [jnp.max(t, axis=0, keepdims=True) for t in st]
        m_new = [jnp.maximum(m_run[hh], jnp.maximum(cmax[hh], cmax[ATT_HEADS + hh])) for hh in nh]
        alpha = [jnp.exp2(m_run[hh] - m_new[hh]) for hh in nh]
        p = [jnp.exp2(st[c] - m_new[hh]) for c, (hh, j) in enumerate(ch)]
        csum = [jnp.sum(t, axis=0, keepdims=True) for t in p]
        l_new = [alpha[hh] * l_run[hh] + csum[hh] + csum[ATT_HEADS + hh] for hh in nh]
        pv = [_dot(jnp.concatenate([vt_scr[j0, hs[hh], :], vt_scr[j0 + 1, hs[hh], :]], axis=1),
                   jnp.concatenate([p[hh].astype(BF16), p[ATT_HEADS + hh].astype(BF16)], axis=0)) for hh in nh]
        for hh in nh:
            acc_scr[hh] = alpha[hh] * acc_scr[hh] + pv[hh]
        return tuple(m_new), tuple(l_new)

    _, l_f = lax.fori_loop(0, (i + 1) // 2, past_pair, (tuple(m0), tuple(l0)))
    for hh in nh:
        att = (acc_scr[hh] * (1.0 / l_f[hh])).T
        mix_scr[:, hs[hh]] = (att * gate_a[hh]).astype(BF16)

    y = _dot(mix_scr[...], w_out_ref[...])
    o_ref[...] = xb + _rms(y, post_g_ref[...])


def _rope_tables(seq):
    pos = jnp.arange(seq, dtype=F32)
    inv = jnp.power(ROPE_THETA, -jnp.arange(0, ROT_DIM, 2, dtype=F32) / ROT_DIM)
    ang = pos[:, None] * inv[None, :]
    cos, sin = jnp.cos(ang), jnp.sin(ang)
    half = ROT_DIM // 2
    zeros = jnp.zeros((seq, HEAD_DIM - ROT_DIM), F32)
    zh = jnp.zeros((seq, half), F32)
    cosf = jnp.concatenate([cos, cos, jnp.ones((seq, HEAD_DIM - ROT_DIM), F32)], axis=1)
    sina = jnp.concatenate([-sin, zh, zeros], axis=1)
    sinb = jnp.concatenate([zh, sin, zeros], axis=1)
    return cosf, sina, sinb


def _even_layer(x, pre_g, post_g, w_in, pool_w, pool_scale, w_out):
    b, s, d = x.shape
    tq = MOBA_BLOCK
    nb = s // tq
    cosf, sina, sinb = _rope_tables(s)
    tab_spec = pl.BlockSpec((tq, HEAD_DIM), lambda bb, ii: (ii, 0))
    x_spec = pl.BlockSpec((None, tq, d), lambda bb, ii: (bb, ii, 0))
    return pl.pallas_call(
        _even_kernel,
        grid=(b, nb),
        in_specs=[x_spec, tab_spec, tab_spec, tab_spec,
                  _const_spec((1, d)), _const_spec(w_in.shape), _const_spec(pool_w.shape),
                  _const_spec((1, POOL_WIDTH)), _const_spec(w_out.shape), _const_spec((1, d))],
        out_specs=x_spec,
        out_shape=jax.ShapeDtypeStruct(x.shape, x.dtype),
        scratch_shapes=[
            pltpu.VMEM((nb, tq, ATT_WIDTH), BF16),
            pltpu.VMEM((nb, ATT_WIDTH, tq), BF16),
            pltpu.VMEM((nb, ATT_WIDTH), F32),
            pltpu.VMEM((ATT_HEADS, nb, tq), F32),
            pltpu.VMEM((ATT_HEADS, HEAD_DIM, tq), F32),
            pltpu.VMEM((POOL_CARRY, POOL_WIDTH), F32),
            pltpu.VMEM((tq, ATT_WIDTH + POOL_WIDTH), BF16),
        ],
        compiler_params=pltpu.CompilerParams(dimension_semantics=("arbitrary", "arbitrary"),
                                             vmem_limit_bytes=VMEM_LIMIT_BYTES),
        name="even_layer",
    )(x, cosf, sina, sinb, pre_g.reshape(1, d), w_in.astype(BF16), pool_w.astype(BF16),
      pool_scale.reshape(1, POOL_WIDTH), w_out.astype(BF16), post_g.reshape(1, d))


def _odd_kernel(zero_ref, x_ref, pre_g_ref, w_in_ref, bb_ref, cc_ref, lamr_ref, lami_ref, d_ref, glu_w_ref, glu_b_ref,
                dwb_ref, ln_g_ref, ln_b_ref, pw_ref, w_out_ref, post_g_ref, o_ref,
                st_re, st_im, h_scr, xs_scr, work_scr, keep_scr, mix_scr, *, nbatch, tl):
    step = pl.program_id(0)
    rows = tl * nbatch
    hist = CONV_HIST * nbatch
    W = S5_WIDTH
    hw = 2 * S5_HALF_LANES
    half_in = S5_HALF_GROUPS * S5_GROUP_IN
    mc = MXU_COLS
    scan_chunks = S5_HALF_LANES // S5_SCAN_LANES

    gext0 = 0
    conv0 = gext0 + hist + rows
    spare0 = conv0 + rows
    su0, gc0, gd0, bu0 = 0, rows, 2 * rows, 3 * rows

    def region(r0):
        return slice(r0, r0 + rows)

    @pl.when(step == 0)
    def _():
        st_re[...] = jnp.zeros_like(st_re)
        st_im[...] = jnp.zeros_like(st_im)
        work_scr[gext0:gext0 + hist, :] = jnp.zeros((hist, W), F32)

    xb = x_ref[...]
    h_scr[...] = _rms(xb, pre_g_ref[...]).astype(BF16)

    def finish(res):
        work_scr[spare0:spare0 + SUBLANES, 0:LANES] = res[rows - SUBLANES:, res.shape[1] - LANES:]

    def project(r0, col0, c):
        def run():
            res = _dot(h_scr[...], w_in_ref[:, col0 + c * mc:col0 + (c + 1) * mc])
            keep_scr[region(r0), c * mc:(c + 1) * mc] = res
            finish(res)
        return run

    def s5_in(hf, c):
        def run():
            su_b = keep_scr[region(su0), hf * half_in:(hf + 1) * half_in].astype(BF16)
            res = _dot(su_b, bb_ref[hf, :, c * S5_SCAN_LANES:(c + 1) * S5_SCAN_LANES])
            keep_scr[region(bu0 + (hf * 2 * scan_chunks + c) * rows), :] = res
            finish(res)
        return run

    mxu_work = [project(su0, 0, c) for c in range(W // mc)]
    mxu_work += [s5_in(hf, c) for hf in range(2) for c in range(2 * scan_chunks)]
    mxu_work += [project(gc0, W, c) for c in range(W // mc)]
    mxu_work += [project(gd0, 2 * W + 2 * CONV_WIDTH, c) for c in range(CONV_WIDTH // mc)]

    ca = _dot(h_scr[...], w_in_ref[:, 2 * W:2 * W + CONV_WIDTH])
    cb = _dot(h_scr[...], w_in_ref[:, 2 * W + CONV_WIDTH:2 * W + 2 * CONV_WIDTH])
    work_scr[gext0 + hist:gext0 + hist + rows, :] = ca * jax.nn.sigmoid(cb)
    lead = CONV_HIST - (CONV_K - 1)
    tiles_per_tok = nbatch // SUBLANES
    ng = CONV_GROUP_TOKENS
    zrow = pl.multiple_of(zero_ref[0] * SUBLANES, SUBLANES)
    n_groups = (CONV_WIDTH // LANES) * tiles_per_tok * (tl // ng)
    n_work = len(mxu_work)
    done = 0
    for lt in range(CONV_WIDTH // LANES):
        ls = slice(lt * LANES, (lt + 1) * LANES)
        for par in range(tiles_per_tok):
            for t0 in range(0, tl, ng):
                acc = [jnp.zeros((SUBLANES, LANES), F32) for _ in range(ng)]
                for k0 in range(0, CONV_K, CONV_TAP_BLOCK):
                    kn = min(CONV_TAP_BLOCK, CONV_K - k0)
                    taps = [dwb_ref[k0 + kk, :, ls] for kk in range(kn)]
                    for u in range(ng + kn - 1):
                        r0 = gext0 + ((t0 + lead + k0 + u) * tiles_per_tok + par) * SUBLANES
                        xin = work_scr[pl.ds(r0 + zrow, SUBLANES), ls]
                        for m in range(max(0, u - kn + 1), min(ng, u + 1)):
                            acc[m] = acc[m] + taps[u - m] * xin
                for m in range(ng):
                    r0 = conv0 + ((t0 + m) * tiles_per_tok + par) * SUBLANES
                    work_scr[r0:r0 + SUBLANES, ls] = acc[m]
                done += 1
                while len(mxu_work) > n_work - (max(0, done - CONV_MXU_SLACK) * n_work) // (n_groups - CONV_MXU_SLACK):
                    mxu_work.pop(0)()
    while mxu_work:
        mxu_work.pop(0)()
    work_scr[gext0:gext0 + hist, :] = work_scr[gext0 + rows:gext0 + rows + hist, :]
    conv = work_scr[region(conv0), :]
    mu = jnp.mean(conv, axis=-1, keepdims=True)
    cen = conv - mu
    var = jnp.mean(cen * cen, axis=-1, keepdims=True)
    nrm = cen * lax.rsqrt(var + EPS) * ln_g_ref[...] + ln_b_ref[...]
    mix_scr[:, W:W + CONV_WIDTH] = (_dot(_silu(nrm).astype(BF16), pw_ref[...])
                                    * _silu(keep_scr[region(gd0), :])).astype(BF16)

    y_half = []
    for c in range(2 * scan_chunks):
        hf, sub = divmod(c, scan_chunks)
        cl = slice(c * S5_SCAN_LANES, (c + 1) * S5_SCAN_LANES)
        re = slice(hf * hw + sub * S5_SCAN_LANES, hf * hw + (sub + 1) * S5_SCAN_LANES)
        im = slice(re.start + S5_HALF_LANES, re.stop + S5_HALF_LANES)
        bu_re = bu0 + (hf * 2 * scan_chunks + sub) * rows
        bu_im = bu_re + scan_chunks * rows
        ar, ai = lamr_ref[:, cl], lami_ref[:, cl]
        sr, si = st_re[:, cl], st_im[:, cl]
        for t in range(tl):
            rs = slice(t * nbatch, (t + 1) * nbatch)
            br = keep_scr[bu_re + t * nbatch:bu_re + (t + 1) * nbatch, :]
            bi = keep_scr[bu_im + t * nbatch:bu_im + (t + 1) * nbatch, :]
            sr, si = ar * sr - ai * si + br, ar * si + ai * sr + bi
            xs_scr[rs, re] = sr.astype(BF16)
            xs_scr[rs, im] = si.astype(BF16)
        st_re[:, cl] = sr
        st_im[:, cl] = si
        if sub == scan_chunks - 1:
            y_half.append(_dot(xs_scr[:, hf * hw:(hf + 1) * hw], cc_ref[hf]))

    y = jnp.concatenate(y_half, axis=1) + d_ref[...] * keep_scr[region(su0), :]
    yy = _dot(y.astype(BF16), glu_w_ref[...]) + glu_b_ref[...]
    mix_scr[:, 0:W] = (yy[:, 0:W] * jax.nn.sigmoid(yy[:, W:2 * W]) * _silu(keep_scr[region(gc0), :])).astype(BF16)

    yo = _dot(mix_scr[...], w_out_ref[...])
    o_ref[...] = xb + _rms(yo, post_g_ref[...])


def _s5_operands(lam_re, lam_im, log_dt, b_re, b_im, c_re, c_im, nbatch):
    lr, li = lam_re.astype(F32), lam_im.astype(F32)
    dt = jnp.exp(log_dt.astype(F32))[:, None]
    mag = jnp.exp(lr * dt)
    ab_re = mag * jnp.cos(li * dt)
    ab_im = mag * jnp.sin(li * dt)
    den = lr * lr + li * li
    num_re = ab_re - 1.0
    f_re = (num_re * lr + ab_im * li) / den
    f_im = (ab_im * lr - num_re * li) / den
    br, bi = b_re.astype(F32), b_im.astype(F32)
    bb_re = f_re[..., None] * br - f_im[..., None] * bi
    bb_im = f_re[..., None] * bi + f_im[..., None] * br
    hg, p, n_in = S5_HALF_GROUPS, S5_STATE, S5_GROUP_IN
    eye = jnp.eye(hg, dtype=F32)

    def pack_b(m):
        return jnp.einsum('hgpj,gk->hgjkp', m.reshape(2, hg, p, n_in), eye).reshape(2, hg * n_in, hg * p)

    def pack_c(m):
        return jnp.einsum('hgip,gk->hgpki', m.reshape(2, hg, n_in, p), eye).reshape(2, hg * p, hg * n_in)

    bb = jnp.concatenate([pack_b(bb_re), pack_b(bb_im)], axis=2).astype(BF16)
    cc = jnp.concatenate([pack_c(c_re.astype(F32)), -pack_c(c_im.astype(F32))], axis=1).astype(BF16)
    lanes = S5_GROUPS * S5_STATE
    lamr = jnp.broadcast_to(ab_re.reshape(1, lanes), (nbatch, lanes))
    lami = jnp.broadcast_to(ab_im.reshape(1, lanes), (nbatch, lanes))
    return bb, cc, lamr, lami


def _odd_layer(xt, nbatch, pre_g, post_g, w_in, lam_re, lam_im, log_dt, b_re, b_im, c_re, c_im, d_skip,
               glu_w, glu_b, dw, ln_g, ln_b, pw, w_out, tl=32):
    n, d = xt.shape
    rows = tl * nbatch
    bb, cc, lamr, lami = _s5_operands(lam_re, lam_im, log_dt, b_re, b_im, c_re, c_im, nbatch)
    lanes = S5_GROUPS * S5_STATE
    x_spec = pl.BlockSpec((rows, d), lambda ss: (ss, 0))
    ops = (xt, pre_g.reshape(1, d), w_in.astype(BF16), bb, cc, lamr, lami, d_skip.reshape(1, S5_WIDTH),
           glu_w.astype(BF16), glu_b.reshape(1, 2 * S5_WIDTH),
           jnp.broadcast_to(dw.astype(F32)[:, None, :], (CONV_K, SUBLANES, CONV_WIDTH)), ln_g.reshape(1, CONV_WIDTH),
           ln_b.reshape(1, CONV_WIDTH), pw.astype(BF16), w_out.astype(BF16), post_g.reshape(1, d))
    return pl.pallas_call(
        functools.partial(_odd_kernel, nbatch=nbatch, tl=tl),
        grid=(n // rows,),
        in_specs=[pl.BlockSpec(memory_space=pltpu.SMEM), x_spec] + [_const_spec(o.shape) for o in ops[1:]],
        out_specs=x_spec,
        out_shape=jax.ShapeDtypeStruct(xt.shape, xt.dtype),
        scratch_shapes=[
            pltpu.VMEM((nbatch, lanes), F32),
            pltpu.VMEM((nbatch, lanes), F32),
            pltpu.VMEM((rows, d), BF16),
            pltpu.VMEM((rows, 2 * lanes), BF16),
            pltpu.VMEM((CONV_HIST * nbatch + 2 * rows + SUBLANES, CONV_WIDTH), F32),
            pltpu.VMEM(((3 + 2 * lanes // S5_SCAN_LANES) * rows, S5_WIDTH), F32),
            pltpu.VMEM((rows, S5_WIDTH + CONV_WIDTH), BF16),
        ],
        compiler_params=pltpu.CompilerParams(dimension_semantics=("arbitrary",),
                                             vmem_limit_bytes=VMEM_LIMIT_BYTES),
        name="odd_layer",
    )(jnp.zeros((1,), jnp.int32), *ops)


def kernel(x, pre_norm_g, post_norm_g, e_w_in, e_pool_w, e_pool_scale, e_w_out, o_w_in, o_lam_re, o_lam_im,
           o_log_dt, o_b_re, o_b_im, o_c_re, o_c_im, o_d, o_glu_w, o_glu_b, o_dw, o_ln_g, o_ln_b, o_pw, o_w_out):
    b, s, d = x.shape
    for i in range(pre_norm_g.shape[0]):
        j = i // 2
        if i % 2 == 0:
            x = _even_layer(x, pre_norm_g[i], post_norm_g[i], e_w_in[j], e_pool_w[j], e_pool_scale[j], e_w_out[j])
        else:
            xt = jnp.transpose(x, (1, 0, 2)).reshape(s * b, d)
            xt = _odd_layer(xt, b, pre_norm_g[i], post_norm_g[i], o_w_in[j], o_lam_re[j], o_lam_im[j],
                            o_log_dt[j], o_b_re[j], o_b_im[j], o_c_re[j], o_c_im[j], o_d[j], o_glu_w[j],
                            o_glu_b[j], o_dw[j], o_ln_g[j], o_ln_b[j], o_pw[j], o_w_out[j])
            x = jnp.transpose(xt.reshape(s, b, d), (1, 0, 2))
    return x
```

```python
import functools
import math

import jax
import jax.numpy as jnp
from jax import lax
from jax.experimental import pallas as pl
from jax.experimental.pallas import tpu as pltpu

F32 = jnp.float32
BF16 = jnp.bfloat16

D_MODEL = 1024
EPS = 1e-6
NEG = -1e30

ATT_WIDTH = 512
HEAD_DIM = 128
ATT_HEADS = 4
ROT_DIM = 32
ROPE_THETA = 500000.0
MOBA_BLOCK = 256
MOBA_TOPK = 3
POOL_WIDTH = 512
POOL_WINDOWS = (2, 4, 8, 16)
POOL_GROUP_W = 128
POOL_CARRY = 16

S5_WIDTH = 512
S5_GROUP_IN = 16
S5_GROUPS = 32
S5_STATE = 64
S5_HALF_GROUPS = 16
S5_HALF_LANES = S5_HALF_GROUPS * S5_STATE
S5_SCAN_LANES = 512
CONV_WIDTH = 512
CONV_K = 31
CONV_HIST = 32
CONV_GROUP_TOKENS = 8
CONV_TAP_BLOCK = 8
CONV_MXU_SLACK = 5

SUBLANES = 8
LANES = 128
MXU_COLS = 256
PERM_ROWS = 256

V7X_VMEM_BYTES = 64 * 1024 * 1024
VMEM_LIMIT_BYTES = 60000 * 1024


def _rms(x, g):
    return x * lax.rsqrt(jnp.mean(x * x, axis=-1, keepdims=True) + EPS) * g


def _silu(x):
    return x * jax.nn.sigmoid(x)


def _dot(a, b):
    return jnp.dot(a, b, preferred_element_type=F32)


def _dot_nt(a, b, precision=None):
    return lax.dot_general(a, b, (((1,), (1,)), ((), ())), preferred_element_type=F32, precision=precision)


def _const_spec(shape):
    nd = len(shape)
    return pl.BlockSpec(shape, lambda *_: (0,) * nd, pipeline_mode=pl.Buffered(1))


def _rope(x, cosf, sina, sinb):
    return x * cosf + pltpu.roll(x, HEAD_DIM - ROT_DIM // 2, axis=1) * sina + pltpu.roll(x, ROT_DIM // 2, axis=1) * sinb


def _even_kernel(x_ref, cos_ref, sina_ref, sinb_ref, pre_g_ref, w_in_ref, pool_w_ref, pool_s_ref, w_out_ref,
                 post_g_ref, o_ref, k_scr, vt_scr, kbar_scr, sel_scr, acc_scr, pcarry_scr, mix_scr):
    i = pl.program_id(1)
    tq = x_ref.shape[0]
    nb = kbar_scr.shape[0]
    A = ATT_WIDTH

    @pl.when(i == 0)
    def _():
        kbar_scr[...] = jnp.zeros_like(kbar_scr)
        pcarry_scr[...] = jnp.zeros_like(pcarry_scr)

    xb = x_ref[...]
    h = _rms(xb, pre_g_ref[...]).astype(BF16)
    q = _dot(h, w_in_ref[:, 0:A])
    k = _dot(h, w_in_ref[:, A:2 * A])
    v = _dot(h, w_in_ref[:, 2 * A:3 * A])
    ga = _dot(h, w_in_ref[:, 3 * A:4 * A])
    pu = _dot(h, w_in_ref[:, 4 * A:4 * A + POOL_WIDTH])
    gb = _dot(h, w_in_ref[:, 4 * A + POOL_WIDTH:4 * A + 2 * POOL_WIDTH])

    cosf, sina, sinb = cos_ref[...], sina_ref[...], sinb_ref[...]
    hs = [slice(hh * HEAD_DIM, (hh + 1) * HEAD_DIM) for hh in range(ATT_HEADS)]
    q_r = [_rope(q[:, s], cosf, sina, sinb) for s in hs]
    k_r = jnp.concatenate([_rope(k[:, s], cosf, sina, sinb) for s in hs], axis=1)
    k_b = k_r.astype(BF16)
    vt_b = v.T.astype(BF16)

    blk = lax.broadcasted_iota(jnp.int32, (nb, 1), 0)
    past = blk < i
    kbar = kbar_scr[...]
    sel = []
    for hh, s in enumerate(hs):
        g = _dot_nt(kbar[:, s], q_r[hh], precision=lax.Precision.HIGHEST)
        g = jnp.where(past, g, -jnp.inf)
        rank = jnp.zeros(g.shape, F32)
        for m in range(nb):
            gm = g[m:m + 1, :]
            beats = (gm > g) | ((gm == g) & (m < blk))
            rank = rank + jnp.where(beats, 1.0, 0.0)
        sel.append(jnp.where(past & (rank < MOBA_TOPK), 1.0, 0.0))

    k_scr[i] = k_b
    vt_scr[i] = vt_b
    kbar_scr[...] = jnp.where(blk == i, jnp.mean(k_r, axis=0, keepdims=True), kbar)

    ext = jnp.concatenate([pcarry_scr[...], pu], axis=0)
    pcarry_scr[...] = pu[tq - POOL_CARRY:, :]
    tpos = i * tq + lax.broadcasted_iota(jnp.int32, (tq, 1), 0)
    for gi, w in enumerate(POOL_WINDOWS):
        cs = slice(gi * POOL_GROUP_W, (gi + 1) * POOL_GROUP_W)
        cum = ext[:, cs]
        d = 1
        while d < w:
            cum = cum + pltpu.roll(cum, d, axis=0)
            d *= 2
        cnt = jnp.minimum(tpos + 1, w).astype(F32)
        mg = cum[POOL_CARRY:, :] / cnt - pu[:, cs]
        yg = _dot(mg.astype(BF16), pool_w_ref[gi]) * pool_s_ref[:, cs] * _silu(gb[:, cs])
        mix_scr[:, A + gi * POOL_GROUP_W:A + (gi + 1) * POOL_GROUP_W] = yg.astype(BF16)

    qscale = math.log2(math.e) / math.sqrt(HEAD_DIM)
    qb = [(qh * qscale).astype(BF16) for qh in q_r]
    gate_a = [_silu(ga[:, s]) for s in hs]
    key_i = lax.broadcasted_iota(jnp.int32, (tq, tq), 0)
    qry_i = lax.broadcasted_iota(jnp.int32, (tq, tq), 1)
    causal = key_i <= qry_i
    nh = range(ATT_HEADS)

    for hh in nh:
        sel_scr[hh] = sel[hh]

    st0 = [jnp.where(causal, _dot_nt(k_b[:, hs[hh]], qb[hh]), NEG) for hh in nh]
    m0 = [jnp.max(st0[hh], axis=0, keepdims=True) for hh in nh]
    p0 = [jnp.exp2(st0[hh] - m0[hh]) for hh in nh]
    l0 = [jnp.sum(p0[hh], axis=0, keepdims=True) for hh in nh]
    for hh in nh:
        acc_scr[hh] = _dot(vt_b[hs[hh], :], p0[hh].astype(BF16))

    def past_pair(jp, carry):
        m_run, l_run = carry
        j0 = 2 * jp
        ch = [(hh, j0 + d) for d in range(2) for hh in nh]
        st = [_dot_nt(k_scr[j, :, hs[hh]], qb[hh]) for hh, j in ch]
        st = [jnp.where(sel_scr[hh, pl.ds(j, 1), :] > 0.5, st[c], NEG) for c, (hh, j) in enumerate(ch)]
        cmax = [jnp.max(t, axis=0, keepdims=True) for t in st]
        m_new = [jnp.maximum(m_run[hh], jnp.maximum(cmax[hh], cmax[ATT_HEADS + hh])) for hh in nh]
        alpha = [jnp.exp2(m_run[hh] - m_new[hh]) for hh in nh]
        p = [jnp.exp2(st[c] - m_new[hh]) for c, (hh, j) in enumerate(ch)]
        csum = [jnp.sum(t, axis=0, keepdims=True) for t in p]
        l_new = [alpha[hh] * l_run[hh] + csum[hh] + csum[ATT_HEADS + hh] for hh in nh]
        pv = [_dot(jnp.concatenate([vt_scr[j0, hs[hh], :], vt_scr[j0 + 1, hs[hh], :]], axis=1),
                   jnp.concatenate([p[hh].astype(BF16), p[ATT_HEADS + hh].astype(BF16)], axis=0)) for hh in nh]
        for hh in nh:
            acc_scr[hh] = alpha[hh] * acc_scr[hh] + pv[hh]
        return tuple(m_new), tuple(l_new)

    _, l_f = lax.fori_loop(0, (i + 1) // 2, past_pair, (tuple(m0), tuple(l0)))
    for hh in nh:
        att = (acc_scr[hh] * (1.0 / l_f[hh])).T
        mix_scr[:, hs[hh]] = (att * gate_a[hh]).astype(BF16)

    y = _dot(mix_scr[...], w_out_ref[...])
    o_ref[...] = xb + _rms(y, post_g_ref[...])


def _rope_tables(seq):
    pos = jnp.arange(seq, dtype=F32)
    inv = jnp.power(ROPE_THETA, -jnp.arange(0, ROT_DIM, 2, dtype=F32) / ROT_DIM)
    ang = pos[:, None] * inv[None, :]
    cos, sin = jnp.cos(ang), jnp.sin(ang)
    half = ROT_DIM // 2
    zeros = jnp.zeros((seq, HEAD_DIM - ROT_DIM), F32)
    zh = jnp.zeros((seq, half), F32)
    cosf = jnp.concatenate([cos, cos, jnp.ones((seq, HEAD_DIM - ROT_DIM), F32)], axis=1)
    sina = jnp.concatenate([-sin, zh, zeros], axis=1)
    sinb = jnp.concatenate([zh, sin, zeros], axis=1)
    return cosf, sina, sinb


def _even_layer(x, pre_g, post_g, w_in, pool_w, pool_scale, w_out):
    b, s, d = x.shape
    tq = MOBA_BLOCK
    nb = s // tq
    cosf, sina, sinb = _rope_tables(s)
    tab_spec = pl.BlockSpec((tq, HEAD_DIM), lambda bb, ii: (ii, 0))
    x_spec = pl.BlockSpec((None, tq, d), lambda bb, ii: (bb, ii, 0))
    return pl.pallas_call(
        _even_kernel,
        grid=(b, nb),
        in_specs=[x_spec, tab_spec, tab_spec, tab_spec,
                  _const_spec((1, d)), _const_spec(w_in.shape), _const_spec(pool_w.shape),
                  _const_spec((1, POOL_WIDTH)), _const_spec(w_out.shape), _const_spec((1, d))],
        out_specs=x_spec,
        out_shape=jax.ShapeDtypeStruct(x.shape, x.dtype),
        scratch_shapes=[
            pltpu.VMEM((nb, tq, ATT_WIDTH), BF16),
            pltpu.VMEM((nb, ATT_WIDTH, tq), BF16),
            pltpu.VMEM((nb, ATT_WIDTH), F32),
            pltpu.VMEM((ATT_HEADS, nb, tq), F32),
            pltpu.VMEM((ATT_HEADS, HEAD_DIM, tq), F32),
            pltpu.VMEM((POOL_CARRY, POOL_WIDTH), F32),
            pltpu.VMEM((tq, ATT_WIDTH + POOL_WIDTH), BF16),
        ],
        compiler_params=pltpu.CompilerParams(dimension_semantics=("arbitrary", "arbitrary"),
                                             vmem_limit_bytes=VMEM_LIMIT_BYTES),
        name="even_layer",
    )(x, cosf, sina, sinb, pre_g.reshape(1, d), w_in.astype(BF16), pool_w.astype(BF16),
      pool_scale.reshape(1, POOL_WIDTH), w_out.astype(BF16), post_g.reshape(1, d))


def _odd_kernel(zero_ref, x_ref, perm_ref, pre_g_ref, w_in_ref, bb_ref, cc_ref, lamr_ref, lami_ref, d_ref, glu_w_ref,
                glu_b_ref, dwb_ref, ln_g_ref, ln_b_ref, pw_ref, w_out_ref, post_g_ref, o_ref,
                st_re, st_im, h_scr, xs_scr, work_scr, keep_scr, mix_scr, mixbt_scr, *, nbatch, tl):
    step = pl.program_id(0)
    rows = tl * nbatch
    hist = CONV_HIST * nbatch
    W = S5_WIDTH
    hw = 2 * S5_HALF_LANES
    half_in = S5_HALF_GROUPS * S5_GROUP_IN
    mc = MXU_COLS
    scan_chunks = S5_HALF_LANES // S5_SCAN_LANES

    gext0 = 0
    conv0 = gext0 + hist + rows
    spare0 = conv0 + rows
    su0, gc0, gd0, bu0 = 0, rows, 2 * rows, 3 * rows

    def region(r0):
        return slice(r0, r0 + rows)

    @pl.when(step == 0)
    def _():
        st_re[...] = jnp.zeros_like(st_re)
        st_im[...] = jnp.zeros_like(st_im)
        work_scr[gext0:gext0 + hist, :] = jnp.zeros((hist, W), F32)

    d_model = x_ref.shape[2]
    tok_per_perm = PERM_ROWS // nbatch
    xb = x_ref[...].reshape(rows, d_model)
    h_bt = _rms(xb, pre_g_ref[...]).astype(BF16).reshape(nbatch, tl, d_model)
    for u in range(tl // tok_per_perm):
        sub = h_bt[:, u * tok_per_perm:(u + 1) * tok_per_perm, :].reshape(PERM_ROWS, d_model)
        h_scr[u * PERM_ROWS:(u + 1) * PERM_ROWS, :] = _dot(perm_ref[0], sub).astype(BF16)

    def finish(res):
        work_scr[spare0:spare0 + SUBLANES, 0:LANES] = res[rows - SUBLANES:, res.shape[1] - LANES:]

    def project(r0, col0, c):
        def run():
            res = _dot(h_scr[...], w_in_ref[:, col0 + c * mc:col0 + (c + 1) * mc])
            keep_scr[region(r0), c * mc:(c + 1) * mc] = res
            finish(res)
        return run

    def s5_in(hf, c):
        def run():
            su_b = keep_scr[region(su0), hf * half_in:(hf + 1) * half_in].astype(BF16)
            res = _dot(su_b, bb_ref[hf, :, c * S5_SCAN_LANES:(c + 1) * S5_SCAN_LANES])
            keep_scr[region(bu0 + (hf * 2 * scan_chunks + c) * rows), :] = res
            finish(res)
        return run

    mxu_work = [project(su0, 0, c) for c in range(W // mc)]
    mxu_work += [s5_in(hf, c) for hf in range(2) for c in range(2 * scan_chunks)]
    mxu_work += [project(gc0, W, c) for c in range(W // mc)]
    mxu_work += [project(gd0, 2 * W + 2 * CONV_WIDTH, c) for c in range(CONV_WIDTH // mc)]

    ca = _dot(h_scr[...], w_in_ref[:, 2 * W:2 * W + CONV_WIDTH])
    cb = _dot(h_scr[...], w_in_ref[:, 2 * W + CONV_WIDTH:2 * W + 2 * CONV_WIDTH])
    work_scr[gext0 + hist:gext0 + hist + rows, :] = ca * jax.nn.sigmoid(cb)
    lead = CONV_HIST - (CONV_K - 1)
    tiles_per_tok = nbatch // SUBLANES
    ng = CONV_GROUP_TOKENS
    zrow = pl.multiple_of(zero_ref[0] * SUBLANES, SUBLANES)
    n_groups = (CONV_WIDTH // LANES) * tiles_per_tok * (tl // ng)
    n_work = len(mxu_work)
    done = 0
    for lt in range(CONV_WIDTH // LANES):
        ls = slice(lt * LANES, (lt + 1) * LANES)
        for par in range(tiles_per_tok):
            for t0 in range(0, tl, ng):
                acc = [jnp.zeros((SUBLANES, LANES), F32) for _ in range(ng)]
                for k0 in range(0, CONV_K, CONV_TAP_BLOCK):
                    kn = min(CONV_TAP_BLOCK, CONV_K - k0)
                    taps = [dwb_ref[k0 + kk, :, ls] for kk in range(kn)]
                    for u in range(ng + kn - 1):
                        r0 = gext0 + ((t0 + lead + k0 + u) * tiles_per_tok + par) * SUBLANES
                        xin = work_scr[pl.ds(r0 + zrow, SUBLANES), ls]
                        for m in range(max(0, u - kn + 1), min(ng, u + 1)):
                            acc[m] = acc[m] + taps[u - m] * xin
                for m in range(ng):
                    r0 = conv0 + ((t0 + m) * tiles_per_tok + par) * SUBLANES
                    work_scr[r0:r0 + SUBLANES, ls] = acc[m]
                done += 1
                while len(mxu_work) > n_work - (max(0, done - CONV_MXU_SLACK) * n_work) // (n_groups - CONV_MXU_SLACK):
                    mxu_work.pop(0)()
    while mxu_work:
        mxu_work.pop(0)()
    work_scr[gext0:gext0 + hist, :] = work_scr[gext0 + rows:gext0 + rows + hist, :]
    conv = work_scr[region(conv0), :]
    mu = jnp.mean(conv, axis=-1, keepdims=True)
    cen = conv - mu
    var = jnp.mean(cen * cen, axis=-1, keepdims=True)
    nrm = cen * lax.rsqrt(var + EPS) * ln_g_ref[...] + ln_b_ref[...]
    mix_scr[:, W:W + CONV_WIDTH] = (_dot(_silu(nrm).astype(BF16), pw_ref[...])
                                    * _silu(keep_scr[region(gd0), :])).astype(BF16)

    y_half = []
    for c in range(2 * scan_chunks):
        hf, sub = divmod(c, scan_chunks)
        cl = slice(c * S5_SCAN_LANES, (c + 1) * S5_SCAN_LANES)
        re = slice(hf * hw + sub * S5_SCAN_LANES, hf * hw + (sub + 1) * S5_SCAN_LANES)
        im = slice(re.start + S5_HALF_LANES, re.stop + S5_HALF_LANES)
        bu_re = bu0 + (hf * 2 * scan_chunks + sub) * rows
        bu_im = bu_re + scan_chunks * rows
        ar, ai = lamr_ref[:, cl], lami_ref[:, cl]
        sr, si = st_re[:, cl], st_im[:, cl]
        for t in range(tl):
            rs = slice(t * nbatch, (t + 1) * nbatch)
            br = keep_scr[bu_re + t * nbatch:bu_re + (t + 1) * nbatch, :]
            bi = keep_scr[bu_im + t * nbatch:bu_im + (t + 1) * nbatch, :]
            sr, si = ar * sr - ai * si + br, ar * si + ai * sr + bi
            xs_scr[rs, re] = sr.astype(BF16)
            xs_scr[rs, im] = si.astype(BF16)
        st_re[:, cl] = sr
        st_im[:, cl] = si
        if sub == scan_chunks - 1:
            y_half.append(_dot(xs_scr[:, hf * hw:(hf + 1) * hw], cc_ref[hf]))

    y = jnp.concatenate(y_half, axis=1) + d_ref[...] * keep_scr[region(su0), :]
    yy = _dot(y.astype(BF16), glu_w_ref[...]) + glu_b_ref[...]
    mix_scr[:, 0:W] = (yy[:, 0:W] * jax.nn.sigmoid(yy[:, W:2 * W]) * _silu(keep_scr[region(gc0), :])).astype(BF16)

    for u in range(tl // tok_per_perm):
        back = _dot(perm_ref[1], mix_scr[u * PERM_ROWS:(u + 1) * PERM_ROWS, :]).astype(BF16)
        mixbt_scr[:, u * tok_per_perm:(u + 1) * tok_per_perm, :] = back.reshape(nbatch, tok_per_perm, W + CONV_WIDTH)
    yo = _dot(mixbt_scr[...].reshape(rows, W + CONV_WIDTH), w_out_ref[...])
    o_ref[...] = (xb + _rms(yo, post_g_ref[...])).reshape(nbatch, tl, d_model)


def _s5_operands(lam_re, lam_im, log_dt, b_re, b_im, c_re, c_im, nbatch):
    lr, li = lam_re.astype(F32), lam_im.astype(F32)
    dt = jnp.exp(log_dt.astype(F32))[:, None]
    mag = jnp.exp(lr * dt)
    ab_re = mag * jnp.cos(li * dt)
    ab_im = mag * jnp.sin(li * dt)
    den = lr * lr + li * li
    num_re = ab_re - 1.0
    f_re = (num_re * lr + ab_im * li) / den
    f_im = (ab_im * lr - num_re * li) / den
    br, bi = b_re.astype(F32), b_im.astype(F32)
    bb_re = f_re[..., None] * br - f_im[..., None] * bi
    bb_im = f_re[..., None] * bi + f_im[..., None] * br
    hg, p, n_in = S5_HALF_GROUPS, S5_STATE, S5_GROUP_IN
    eye = jnp.eye(hg, dtype=F32)

    def pack_b(m):
        return jnp.einsum('hgpj,gk->hgjkp', m.reshape(2, hg, p, n_in), eye).reshape(2, hg * n_in, hg * p)

    def pack_c(m):
        return jnp.einsum('hgip,gk->hgpki', m.reshape(2, hg, n_in, p), eye).reshape(2, hg * p, hg * n_in)

    bb = jnp.concatenate([pack_b(bb_re), pack_b(bb_im)], axis=2).astype(BF16)
    cc = jnp.concatenate([pack_c(c_re.astype(F32)), -pack_c(c_im.astype(F32))], axis=1).astype(BF16)
    lanes = S5_GROUPS * S5_STATE
    lamr = jnp.broadcast_to(ab_re.reshape(1, lanes), (nbatch, lanes))
    lami = jnp.broadcast_to(ab_im.reshape(1, lanes), (nbatch, lanes))
    return bb, cc, lamr, lami


def _row_permutations(nbatch):
    tok = PERM_ROWS // nbatch
    dst = jnp.arange(PERM_ROWS)
    src = (dst % nbatch) * tok + dst // nbatch
    fwd = (src[:, None] == jnp.arange(PERM_ROWS)[None, :]).astype(BF16)
    return jnp.stack([fwd, fwd.T])


def _odd_layer(x, pre_g, post_g, w_in, lam_re, lam_im, log_dt, b_re, b_im, c_re, c_im, d_skip,
               glu_w, glu_b, dw, ln_g, ln_b, pw, w_out, tl=32):
    nbatch, seq, d = x.shape
    rows = tl * nbatch
    bb, cc, lamr, lami = _s5_operands(lam_re, lam_im, log_dt, b_re, b_im, c_re, c_im, nbatch)
    lanes = S5_GROUPS * S5_STATE
    x_spec = pl.BlockSpec((nbatch, tl, d), lambda ss: (0, ss, 0))
    ops = (x, _row_permutations(nbatch), pre_g.reshape(1, d), w_in.astype(BF16), bb, cc, lamr, lami,
           d_skip.reshape(1, S5_WIDTH),
           glu_w.astype(BF16), glu_b.reshape(1, 2 * S5_WIDTH),
           jnp.broadcast_to(dw.astype(F32)[:, None, :], (CONV_K, SUBLANES, CONV_WIDTH)), ln_g.reshape(1, CONV_WIDTH),
           ln_b.reshape(1, CONV_WIDTH), pw.astype(BF16), w_out.astype(BF16), post_g.reshape(1, d))
    return pl.pallas_call(
        functools.partial(_odd_kernel, nbatch=nbatch, tl=tl),
        grid=(seq // tl,),
        in_specs=[pl.BlockSpec(memory_space=pltpu.SMEM), x_spec] + [_const_spec(o.shape) for o in ops[1:]],
        out_specs=x_spec,
        out_shape=jax.ShapeDtypeStruct(x.shape, x.dtype),
        scratch_shapes=[
            pltpu.VMEM((nbatch, lanes), F32),
            pltpu.VMEM((nbatch, lanes), F32),
            pltpu.VMEM((rows, d), BF16),
            pltpu.VMEM((rows, 2 * lanes), BF16),
            pltpu.VMEM((CONV_HIST * nbatch + 2 * rows + SUBLANES, CONV_WIDTH), F32),
            pltpu.VMEM(((3 + 2 * lanes // S5_SCAN_LANES) * rows, S5_WIDTH), F32),
            pltpu.VMEM((rows, S5_WIDTH + CONV_WIDTH), BF16),
            pltpu.VMEM((nbatch, tl, S5_WIDTH + CONV_WIDTH), BF16),
        ],
        compiler_params=pltpu.CompilerParams(dimension_semantics=("arbitrary",),
                                             vmem_limit_bytes=VMEM_LIMIT_BYTES),
        name="odd_layer",
    )(jnp.zeros((1,), jnp.int32), *ops)


def kernel(x, pre_norm_g, post_norm_g, e_w_in, e_pool_w, e_pool_scale, e_w_out, o_w_in, o_lam_re, o_lam_im,
           o_log_dt, o_b_re, o_b_im, o_c_re, o_c_im, o_d, o_glu_w, o_glu_b, o_dw, o_ln_g, o_ln_b, o_pw, o_w_out):
    b, s, d = x.shape
    for i in range(pre_norm_g.shape[0]):
        j = i // 2
        if i % 2 == 0:
            x = _even_layer(x, pre_norm_g[i], post_norm_g[i], e_w_in[j], e_pool_w[j], e_pool_scale[j], e_w_out[j])
        else:
            x = _odd_layer(x, pre_norm_g[i], post_norm_g[i], o_w_in[j], o_lam_re[j], o_lam_im[j],
                           o_log_dt[j], o_b_re[j], o_b_im[j], o_c_re[j], o_c_im[j], o_d[j], o_glu_w[j],
                           o_glu_b[j], o_dw[j], o_ln_g[j], o_ln_b[j], o_pw[j], o_w_out[j])
    return x
```

```python
import functools
import math

import jax
import jax.numpy as jnp
from jax import lax
from jax.experimental import pallas as pl
from jax.experimental.pallas import tpu as pltpu

F32 = jnp.float32
BF16 = jnp.bfloat16

D_MODEL = 1024
EPS = 1e-6
NEG = -1e30

ATT_WIDTH = 512
HEAD_DIM = 128
ATT_HEADS = 4
ROT_DIM = 32
ROPE_THETA = 500000.0
MOBA_BLOCK = 256
MOBA_TOPK = 3
POOL_WIDTH = 512
POOL_WINDOWS = (2, 4, 8, 16)
POOL_GROUP_W = 128
POOL_CARRY = 16

S5_WIDTH = 512
S5_GROUP_IN = 16
S5_GROUPS = 32
S5_STATE = 64
S5_HALF_GROUPS = 16
S5_HALF_LANES = S5_HALF_GROUPS * S5_STATE
S5_SCAN_LANES = 512
CONV_WIDTH = 512
CONV_K = 31
CONV_HIST = 32
CONV_GROUP_TOKENS = 8
CONV_TAP_BLOCK = 8
CONV_MXU_SLACK = 5

SUBLANES = 8
LANES = 128
MXU_COLS = 256
PERM_ROWS = 256

V7X_VMEM_BYTES = 64 * 1024 * 1024
VMEM_LIMIT_BYTES = 60000 * 1024


def _rms(x, g):
    return x * lax.rsqrt(jnp.mean(x * x, axis=-1, keepdims=True) + EPS) * g


def _silu(x):
    return x * jax.nn.sigmoid(x)


def _dot(a, b):
    return jnp.dot(a, b, preferred_element_type=F32)


def _dot_nt(a, b, precision=None):
    return lax.dot_general(a, b, (((1,), (1,)), ((), ())), preferred_element_type=F32, precision=precision)


def _const_spec(shape):
    nd = len(shape)
    return pl.BlockSpec(shape, lambda *_: (0,) * nd, pipeline_mode=pl.Buffered(1))


def _rope(x, cosf, sina, sinb):
    return x * cosf + pltpu.roll(x, HEAD_DIM - ROT_DIM // 2, axis=1) * sina + pltpu.roll(x, ROT_DIM // 2, axis=1) * sinb


def _even_kernel(x_ref, cos_ref, sina_ref, sinb_ref, pre_g_ref, w_in_ref, pool_w_ref, pool_s_ref, w_out_ref,
                 post_g_ref, o_ref, k_scr, vt_scr, kbar_scr, sel_scr, acc_scr, pcarry_scr, mix_scr):
    step = pl.program_id(1)
    tq = x_ref.shape[0]
    bs = MOBA_BLOCK
    nb = kbar_scr.shape[0]
    A = ATT_WIDTH
    first, second = slice(0, bs), slice(bs, 2 * bs)
    blk_a = 2 * step

    @pl.when(step == 0)
    def _():
        kbar_scr[...] = jnp.zeros_like(kbar_scr)
        pcarry_scr[...] = jnp.zeros_like(pcarry_scr)

    xb = x_ref[...]
    h = _rms(xb, pre_g_ref[...]).astype(BF16)
    q = _dot(h, w_in_ref[:, 0:A])
    k = _dot(h, w_in_ref[:, A:2 * A])
    v = _dot(h, w_in_ref[:, 2 * A:3 * A])
    ga = _dot(h, w_in_ref[:, 3 * A:4 * A])
    pu = _dot(h, w_in_ref[:, 4 * A:4 * A + POOL_WIDTH])
    gb = _dot(h, w_in_ref[:, 4 * A + POOL_WIDTH:4 * A + 2 * POOL_WIDTH])

    cosf, sina, sinb = cos_ref[...], sina_ref[...], sinb_ref[...]
    hs = [slice(hh * HEAD_DIM, (hh + 1) * HEAD_DIM) for hh in range(ATT_HEADS)]
    q_r = [_rope(q[:, s], cosf, sina, sinb) for s in hs]
    k_r = jnp.concatenate([_rope(k[:, s], cosf, sina, sinb) for s in hs], axis=1)
    k_b = k_r.astype(BF16)
    vt_b = v.T.astype(BF16)

    blk = lax.broadcasted_iota(jnp.int32, (nb, 1), 0)
    kbar = jnp.where(blk == blk_a, jnp.mean(k_r[first, :], axis=0, keepdims=True),
                     jnp.where(blk == blk_a + 1, jnp.mean(k_r[second, :], axis=0, keepdims=True), kbar_scr[...]))
    kbar_scr[...] = kbar
    k_scr[blk_a] = k_b[first, :]
    k_scr[blk_a + 1] = k_b[second, :]
    vt_scr[blk_a] = vt_b[:, first]
    vt_scr[blk_a + 1] = vt_b[:, second]

    qblk = blk_a + jnp.where(lax.broadcasted_iota(jnp.int32, (1, tq), 1) >= bs, 1, 0)
    past = blk < qblk
    for hh, s in enumerate(hs):
        g = _dot_nt(kbar[:, s], q_r[hh], precision=lax.Precision.HIGHEST)
        g = jnp.where(past, g, -jnp.inf)
        rank = jnp.zeros(g.shape, F32)
        for m in range(nb):
            gm = g[m:m + 1, :]
            beats = (gm > g) | ((gm == g) & (m < blk))
            rank = rank + jnp.where(beats, 1.0, 0.0)
        sel_scr[hh] = jnp.where(past & (rank < MOBA_TOPK), 1.0, 0.0)

    ext = jnp.concatenate([pcarry_scr[...], pu], axis=0)
    pcarry_scr[...] = pu[tq - POOL_CARRY:, :]
    tpos = step * tq + lax.broadcasted_iota(jnp.int32, (tq, 1), 0)
    for gi, w in enumerate(POOL_WINDOWS):
        cs = slice(gi * POOL_GROUP_W, (gi + 1) * POOL_GROUP_W)
        cum = ext[:, cs]
        d = 1
        while d < w:
            cum = cum + pltpu.roll(cum, d, axis=0)
            d *= 2
        cnt = jnp.minimum(tpos + 1, w).astype(F32)
        mg = cum[POOL_CARRY:, :] / cnt - pu[:, cs]
        yg = _dot(mg.astype(BF16), pool_w_ref[gi]) * pool_s_ref[:, cs] * _silu(gb[:, cs])
        mix_scr[:, A + gi * POOL_GROUP_W:A + (gi + 1) * POOL_GROUP_W] = yg.astype(BF16)

    qscale = math.log2(math.e) / math.sqrt(HEAD_DIM)
    qb = [(qh * qscale).astype(BF16) for qh in q_r]
    gate_a = [_silu(ga[:, s]) for s in hs]
    key_i = lax.broadcasted_iota(jnp.int32, (bs, bs), 0)
    qry_i = lax.broadcasted_iota(jnp.int32, (bs, bs), 1)
    causal = key_i <= qry_i
    nh = range(ATT_HEADS)

    sel_ab = [sel_scr[hh, pl.ds(blk_a, 1), second] > 0.5 for hh in nh]
    t_aa = [jnp.where(causal, _dot_nt(k_b[first, hs[hh]], qb[hh][first, :]), NEG) for hh in nh]
    t_ab = [jnp.where(sel_ab[hh], _dot_nt(k_b[first, hs[hh]], qb[hh][second, :]), NEG) for hh in nh]
    t_bb = [jnp.where(causal, _dot_nt(k_b[second, hs[hh]], qb[hh][second, :]), NEG) for hh in nh]
    m_a = [jnp.max(t_aa[hh], axis=0, keepdims=True) for hh in nh]
    m_b = [jnp.maximum(jnp.max(t_ab[hh], axis=0, keepdims=True), jnp.max(t_bb[hh], axis=0, keepdims=True)) for hh in nh]
    p_aa = [jnp.exp2(t_aa[hh] - m_a[hh]) for hh in nh]
    p_ab = [jnp.exp2(t_ab[hh] - m_b[hh]) for hh in nh]
    p_bb = [jnp.exp2(t_bb[hh] - m_b[hh]) for hh in nh]
    m0 = [jnp.concatenate([m_a[hh], m_b[hh]], axis=1) for hh in nh]
    l0 = [jnp.concatenate([jnp.sum(p_aa[hh], axis=0, keepdims=True),
                           jnp.sum(p_ab[hh], axis=0, keepdims=True) + jnp.sum(p_bb[hh], axis=0, keepdims=True)],
                          axis=1) for hh in nh]
    for hh in nh:
        acc_scr[hh, :, first] = _dot(vt_b[hs[hh], first], p_aa[hh].astype(BF16))
        acc_scr[hh, :, second] = _dot(vt_b[hs[hh], :],
                                      jnp.concatenate([p_ab[hh].astype(BF16), p_bb[hh].astype(BF16)], axis=0))

    def past_pair(jp, carry):
        m_run, l_run = carry
        j0 = 2 * jp
        ch = [(hh, j0 + d) for d in range(2) for hh in nh]
        st = [_dot_nt(k_scr[j, :, hs[hh]], qb[hh]) for hh, j in ch]
        st = [jnp.where(sel_scr[hh, pl.ds(j, 1), :] > 0.5, st[c], NEG) for c, (hh, j) in enumerate(ch)]
        cmax = [jnp.max(t, axis=0, keepdims=True) for t in st]
        m_new = [jnp.maximum(m_run[hh], jnp.maximum(cmax[hh], cmax[ATT_HEADS + hh])) for hh in nh]
        alpha = [jnp.exp2(m_run[hh] - m_new[hh]) for hh in nh]
        p = [jnp.exp2(st[c] - m_new[hh]) for c, (hh, j) in enumerate(ch)]
        csum = [jnp.sum(t, axis=0, keepdims=True) for t in p]
        l_new = [alpha[hh] * l_run[hh] + csum[hh] + csum[ATT_HEADS + hh] for hh in nh]
        pv = [_dot(jnp.concatenate([vt_scr[j0, hs[hh], :], vt_scr[j0 + 1, hs[hh], :]], axis=1),
                   jnp.concatenate([p[hh].astype(BF16), p[ATT_HEADS + hh].astype(BF16)], axis=0)) for hh in nh]
        for hh in nh:
            acc_scr[hh] = alpha[hh] * acc_scr[hh] + pv[hh]
        return tuple(m_new), tuple(l_new)

    _, l_f = lax.fori_loop(0, step, past_pair, (tuple(m0), tuple(l0)))
    for hh in nh:
        att = (acc_scr[hh] * (1.0 / l_f[hh])).T
        mix_scr[:, hs[hh]] = (att * gate_a[hh]).astype(BF16)

    y = _dot(mix_scr[...], w_out_ref[...])
    o_ref[...] = xb + _rms(y, post_g_ref[...])


def _rope_tables(seq):
    pos = jnp.arange(seq, dtype=F32)
    inv = jnp.power(ROPE_THETA, -jnp.arange(0, ROT_DIM, 2, dtype=F32) / ROT_DIM)
    ang = pos[:, None] * inv[None, :]
    cos, sin = jnp.cos(ang), jnp.sin(ang)
    half = ROT_DIM // 2
    zeros = jnp.zeros((seq, HEAD_DIM - ROT_DIM), F32)
    zh = jnp.zeros((seq, half), F32)
    cosf = jnp.concatenate([cos, cos, jnp.ones((seq, HEAD_DIM - ROT_DIM), F32)], axis=1)
    sina = jnp.concatenate([-sin, zh, zeros], axis=1)
    sinb = jnp.concatenate([zh, sin, zeros], axis=1)
    return cosf, sina, sinb


def _even_layer(x, pre_g, post_g, w_in, pool_w, pool_scale, w_out):
    b, s, d = x.shape
    bs = MOBA_BLOCK
    tq = 2 * bs
    nb = s // bs
    cosf, sina, sinb = _rope_tables(s)
    tab_spec = pl.BlockSpec((tq, HEAD_DIM), lambda bb, ii: (ii, 0))
    x_spec = pl.BlockSpec((None, tq, d), lambda bb, ii: (bb, ii, 0))
    return pl.pallas_call(
        _even_kernel,
        grid=(b, s // tq),
        in_specs=[x_spec, tab_spec, tab_spec, tab_spec,
                  _const_spec((1, d)), _const_spec(w_in.shape), _const_spec(pool_w.shape),
                  _const_spec((1, POOL_WIDTH)), _const_spec(w_out.shape), _const_spec((1, d))],
        out_specs=x_spec,
        out_shape=jax.ShapeDtypeStruct(x.shape, x.dtype),
        scratch_shapes=[
            pltpu.VMEM((nb, bs, ATT_WIDTH), BF16),
            pltpu.VMEM((nb, ATT_WIDTH, bs), BF16),
            pltpu.VMEM((nb, ATT_WIDTH), F32),
            pltpu.VMEM((ATT_HEADS, nb, tq), F32),
            pltpu.VMEM((ATT_HEADS, HEAD_DIM, tq), F32),
            pltpu.VMEM((POOL_CARRY, POOL_WIDTH), F32),
            pltpu.VMEM((tq, ATT_WIDTH + POOL_WIDTH), BF16),
        ],
        compiler_params=pltpu.CompilerParams(dimension_semantics=("arbitrary", "arbitrary"),
                                             vmem_limit_bytes=VMEM_LIMIT_BYTES),
        name="even_layer",
    )(x, cosf, sina, sinb, pre_g.reshape(1, d), w_in.astype(BF16), pool_w.astype(BF16),
      pool_scale.reshape(1, POOL_WIDTH), w_out.astype(BF16), post_g.reshape(1, d))


def _odd_kernel(zero_ref, x_ref, perm_ref, pre_g_ref, w_in_ref, bb_ref, cc_ref, lamr_ref, lami_ref, d_ref, glu_w_ref,
                glu_b_ref, dwb_ref, ln_g_ref, ln_b_ref, pw_ref, w_out_ref, post_g_ref, o_ref,
                st_re, st_im, h_scr, xs_scr, work_scr, keep_scr, mix_scr, mixbt_scr, *, nbatch, tl):
    step = pl.program_id(0)
    rows = tl * nbatch
    hist = CONV_HIST * nbatch
    W = S5_WIDTH
    hw = 2 * S5_HALF_LANES
    half_in = S5_HALF_GROUPS * S5_GROUP_IN
    mc = MXU_COLS
    scan_chunks = S5_HALF_LANES // S5_SCAN_LANES

    gext0 = 0
    conv0 = gext0 + hist + rows
    spare0 = conv0 + rows
    su0, gc0, gd0, bu0 = 0, rows, 2 * rows, 3 * rows

    def region(r0):
        return slice(r0, r0 + rows)

    @pl.when(step == 0)
    def _():
        st_re[...] = jnp.zeros_like(st_re)
        st_im[...] = jnp.zeros_like(st_im)
        work_scr[gext0:gext0 + hist, :] = jnp.zeros((hist, W), F32)

    d_model = x_ref.shape[2]
    tok_per_perm = PERM_ROWS // nbatch
    xb = x_ref[...].reshape(rows, d_model)
    h_bt = _rms(xb, pre_g_ref[...]).astype(BF16).reshape(nbatch, tl, d_model)
    for u in range(tl // tok_per_perm):
        sub = h_bt[:, u * tok_per_perm:(u + 1) * tok_per_perm, :].reshape(PERM_ROWS, d_model)
        h_scr[u * PERM_ROWS:(u + 1) * PERM_ROWS, :] = _dot(perm_ref[0], sub).astype(BF16)

    def finish(res):
        work_scr[spare0:spare0 + SUBLANES, 0:LANES] = res[rows - SUBLANES:, res.shape[1] - LANES:]

    def project(r0, col0, c):
        def run():
            res = _dot(h_scr[...], w_in_ref[:, col0 + c * mc:col0 + (c + 1) * mc])
            keep_scr[region(r0), c * mc:(c + 1) * mc] = res
            finish(res)
        return run

    def s5_in(hf, c):
        def run():
            su_b = keep_scr[region(su0), hf * half_in:(hf + 1) * half_in].astype(BF16)
            res = _dot(su_b, bb_ref[hf, :, c * S5_SCAN_LANES:(c + 1) * S5_SCAN_LANES])
            keep_scr[region(bu0 + (hf * 2 * scan_chunks + c) * rows), :] = res
            finish(res)
        return run

    mxu_work = [project(su0, 0, c) for c in range(W // mc)]
    mxu_work += [s5_in(hf, c) for hf in range(2) for c in range(2 * scan_chunks)]
    mxu_work += [project(gc0, W, c) for c in range(W // mc)]
    mxu_work += [project(gd0, 2 * W + 2 * CONV_WIDTH, c) for c in range(CONV_WIDTH // mc)]

    ca = _dot(h_scr[...], w_in_ref[:, 2 * W:2 * W + CONV_WIDTH])
    cb = _dot(h_scr[...], w_in_ref[:, 2 * W + CONV_WIDTH:2 * W + 2 * CONV_WIDTH])
    work_scr[gext0 + hist:gext0 + hist + rows, :] = ca * jax.nn.sigmoid(cb)
    lead = CONV_HIST - (CONV_K - 1)
    tiles_per_tok = nbatch // SUBLANES
    ng = CONV_GROUP_TOKENS
    zrow = pl.multiple_of(zero_ref[0] * SUBLANES, SUBLANES)
    n_groups = (CONV_WIDTH // LANES) * tiles_per_tok * (tl // ng)
    n_work = len(mxu_work)
    done = 0
    for lt in range(CONV_WIDTH // LANES):
        ls = slice(lt * LANES, (lt + 1) * LANES)
        for par in range(tiles_per_tok):
            for t0 in range(0, tl, ng):
                acc = [jnp.zeros((SUBLANES, LANES), F32) for _ in range(ng)]
                for k0 in range(0, CONV_K, CONV_TAP_BLOCK):
                    kn = min(CONV_TAP_BLOCK, CONV_K - k0)
                    taps = [dwb_ref[k0 + kk, :, ls] for kk in range(kn)]
                    for u in range(ng + kn - 1):
                        r0 = gext0 + ((t0 + lead + k0 + u) * tiles_per_tok + par) * SUBLANES
                        xin = work_scr[pl.ds(r0 + zrow, SUBLANES), ls]
                        for m in range(max(0, u - kn + 1), min(ng, u + 1)):
                            acc[m] = acc[m] + taps[u - m] * xin
                for m in range(ng):
                    r0 = conv0 + ((t0 + m) * tiles_per_tok + par) * SUBLANES
                    work_scr[r0:r0 + SUBLANES, ls] = acc[m]
                done += 1
                while len(mxu_work) > n_work - (max(0, done - CONV_MXU_SLACK) * n_work) // (n_groups - CONV_MXU_SLACK):
                    mxu_work.pop(0)()
    while mxu_work:
        mxu_work.pop(0)()
    work_scr[gext0:gext0 + hist, :] = work_scr[gext0 + rows:gext0 + rows + hist, :]
    conv = work_scr[region(conv0), :]
    mu = jnp.mean(conv, axis=-1, keepdims=True)
    cen = conv - mu
    var = jnp.mean(cen * cen, axis=-1, keepdims=True)
    nrm = cen * lax.rsqrt(var + EPS) * ln_g_ref[...] + ln_b_ref[...]
    mix_scr[:, W:W + CONV_WIDTH] = (_dot(_silu(nrm).astype(BF16), pw_ref[...])
                                    * _silu(keep_scr[region(gd0), :])).astype(BF16)

    y_half = []
    for c in range(2 * scan_chunks):
        hf, sub = divmod(c, scan_chunks)
        cl = slice(c * S5_SCAN_LANES, (c + 1) * S5_SCAN_LANES)
        re = slice(hf * hw + sub * S5_SCAN_LANES, hf * hw + (sub + 1) * S5_SCAN_LANES)
        im = slice(re.start + S5_HALF_LANES, re.stop + S5_HALF_LANES)
        bu_re = bu0 + (hf * 2 * scan_chunks + sub) * rows
        bu_im = bu_re + scan_chunks * rows
        ar, ai = lamr_ref[:, cl], lami_ref[:, cl]
        sr, si = st_re[:, cl], st_im[:, cl]
        for t in range(tl):
            rs = slice(t * nbatch, (t + 1) * nbatch)
            br = keep_scr[bu_re + t * nbatch:bu_re + (t + 1) * nbatch, :]
            bi = keep_scr[bu_im + t * nbatch:bu_im + (t + 1) * nbatch, :]
            sr, si = ar * sr - ai * si + br, ar * si + ai * sr + bi
            xs_scr[rs, re] = sr.astype(BF16)
            xs_scr[rs, im] = si.astype(BF16)
        st_re[:, cl] = sr
        st_im[:, cl] = si
        if sub == scan_chunks - 1:
            y_half.append(_dot(xs_scr[:, hf * hw:(hf + 1) * hw], cc_ref[hf]))

    y = jnp.concatenate(y_half, axis=1) + d_ref[...] * keep_scr[region(su0), :]
    yy = _dot(y.astype(BF16), glu_w_ref[...]) + glu_b_ref[...]
    mix_scr[:, 0:W] = (yy[:, 0:W] * jax.nn.sigmoid(yy[:, W:2 * W]) * _silu(keep_scr[region(gc0), :])).astype(BF16)

    for u in range(tl // tok_per_perm):
        back = _dot(perm_ref[1], mix_scr[u * PERM_ROWS:(u + 1) * PERM_ROWS, :]).astype(BF16)
        mixbt_scr[:, u * tok_per_perm:(u + 1) * tok_per_perm, :] = back.reshape(nbatch, tok_per_perm, W + CONV_WIDTH)
    yo = _dot(mixbt_scr[...].reshape(rows, W + CONV_WIDTH), w_out_ref[...])
    o_ref[...] = (xb + _rms(yo, post_g_ref[...])).reshape(nbatch, tl, d_model)


def _s5_operands(lam_re, lam_im, log_dt, b_re, b_im, c_re, c_im, nbatch):
    lr, li = lam_re.astype(F32), lam_im.astype(F32)
    dt = jnp.exp(log_dt.astype(F32))[:, None]
    mag = jnp.exp(lr * dt)
    ab_re = mag * jnp.cos(li * dt)
    ab_im = mag * jnp.sin(li * dt)
    den = lr * lr + li * li
    num_re = ab_re - 1.0
    f_re = (num_re * lr + ab_im * li) / den
    f_im = (ab_im * lr - num_re * li) / den
    br, bi = b_re.astype(F32), b_im.astype(F32)
    bb_re = f_re[..., None] * br - f_im[..., None] * bi
    bb_im = f_re[..., None] * bi + f_im[..., None] * br
    hg, p, n_in = S5_HALF_GROUPS, S5_STATE, S5_GROUP_IN
    eye = jnp.eye(hg, dtype=F32)

    def pack_b(m):
        return jnp.einsum('hgpj,gk->hgjkp', m.reshape(2, hg, p, n_in), eye).reshape(2, hg * n_in, hg * p)

    def pack_c(m):
        return jnp.einsum('hgip,gk->hgpki', m.reshape(2, hg, n_in, p), eye).reshape(2, hg * p, hg * n_in)

    bb = jnp.concatenate([pack_b(bb_re), pack_b(bb_im)], axis=2).astype(BF16)
    cc = jnp.concatenate([pack_c(c_re.astype(F32)), -pack_c(c_im.astype(F32))], axis=1).astype(BF16)
    lanes = S5_GROUPS * S5_STATE
    lamr = jnp.broadcast_to(ab_re.reshape(1, lanes), (nbatch, lanes))
    lami = jnp.broadcast_to(ab_im.reshape(1, lanes), (nbatch, lanes))
    return bb, cc, lamr, lami


def _row_permutations(nbatch):
    tok = PERM_ROWS // nbatch
    dst = jnp.arange(PERM_ROWS)
    src = (dst % nbatch) * tok + dst // nbatch
    fwd = (src[:, None] == jnp.arange(PERM_ROWS)[None, :]).astype(BF16)
    return jnp.stack([fwd, fwd.T])


def _odd_layer(x, pre_g, post_g, w_in, lam_re, lam_im, log_dt, b_re, b_im, c_re, c_im, d_skip,
               glu_w, glu_b, dw, ln_g, ln_b, pw, w_out, tl=32):
    nbatch, seq, d = x.shape
    rows = tl * nbatch
    bb, cc, lamr, lami = _s5_operands(lam_re, lam_im, log_dt, b_re, b_im, c_re, c_im, nbatch)
    lanes = S5_GROUPS * S5_STATE
    x_spec = pl.BlockSpec((nbatch, tl, d), lambda ss: (0, ss, 0))
    ops = (x, _row_permutations(nbatch), pre_g.reshape(1, d), w_in.astype(BF16), bb, cc, lamr, lami,
           d_skip.reshape(1, S5_WIDTH),
           glu_w.astype(BF16), glu_b.reshape(1, 2 * S5_WIDTH),
           jnp.broadcast_to(dw.astype(F32)[:, None, :], (CONV_K, SUBLANES, CONV_WIDTH)), ln_g.reshape(1, CONV_WIDTH),
           ln_b.reshape(1, CONV_WIDTH), pw.astype(BF16), w_out.astype(BF16), post_g.reshape(1, d))
    return pl.pallas_call(
        functools.partial(_odd_kernel, nbatch=nbatch, tl=tl),
        grid=(seq // tl,),
        in_specs=[pl.BlockSpec(memory_space=pltpu.SMEM), x_spec] + [_const_spec(o.shape) for o in ops[1:]],
        out_specs=x_spec,
        out_shape=jax.ShapeDtypeStruct(x.shape, x.dtype),
        scratch_shapes=[
            pltpu.VMEM((nbatch, lanes), F32),
            pltpu.VMEM((nbatch, lanes), F32),
            pltpu.VMEM((rows, d), BF16),
            pltpu.VMEM((rows, 2 * lanes), BF16),
            pltpu.VMEM((CONV_HIST * nbatch + 2 * rows + SUBLANES, CONV_WIDTH), F32),
            pltpu.VMEM(((3 + 2 * lanes // S5_SCAN_LANES) * rows, S5_WIDTH), F32),
            pltpu.VMEM((rows, S5_WIDTH + CONV_WIDTH), BF16),
            pltpu.VMEM((nbatch, tl, S5_WIDTH + CONV_WIDTH), BF16),
        ],
        compiler_params=pltpu.CompilerParams(dimension_semantics=("arbitrary",),
                                             vmem_limit_bytes=VMEM_LIMIT_BYTES),
        name="odd_layer",
    )(jnp.zeros((1,), jnp.int32), *ops)


def kernel(x, pre_norm_g, post_norm_g, e_w_in, e_pool_w, e_pool_scale, e_w_out, o_w_in, o_lam_re, o_lam_im,
           o_log_dt, o_b_re, o_b_im, o_c_re, o_c_im, o_d, o_glu_w, o_glu_b, o_dw, o_ln_g, o_ln_b, o_pw, o_w_out):
    b, s, d = x.shape
    for i in range(pre_norm_g.shape[0]):
        j = i // 2
        if i % 2 == 0:
            x = _even_layer(x, pre_norm_g[i], post_norm_g[i], e_w_in[j], e_pool_w[j], e_pool_scale[j], e_w_out[j])
        else:
            x = _odd_layer(x, pre_norm_g[i], post_norm_g[i], o_w_in[j], o_lam_re[j], o_lam_im[j],
                           o_log_dt[j], o_b_re[j], o_b_im[j], o_c_re[j], o_c_im[j], o_d[j], o_glu_w[j],
                           o_glu_b[j], o_dw[j], o_ln_g[j], o_ln_b[j], o_pw[j], o_w_out[j])
    return x
```

```python
import functools
import math

import jax
import jax.numpy as jnp
from jax import lax
from jax.experimental import pallas as pl
from jax.experimental.pallas import tpu as pltpu

F32 = jnp.float32
BF16 = jnp.bfloat16

D_MODEL = 1024
EPS = 1e-6
NEG = -1e30

ATT_WIDTH = 512
HEAD_DIM = 128
ATT_HEADS = 4
ROT_DIM = 32
ROPE_THETA = 500000.0
MOBA_BLOCK = 256
MOBA_TOPK = 3
POOL_WIDTH = 512
POOL_WINDOWS = (2, 4, 8, 16)
POOL_GROUP_W = 128
POOL_CARRY = 16

S5_WIDTH = 512
S5_GROUP_IN = 16
S5_GROUPS = 32
S5_STATE = 64
S5_HALF_GROUPS = 16
S5_HALF_LANES = S5_HALF_GROUPS * S5_STATE
S5_SCAN_LANES = 512
CONV_WIDTH = 512
CONV_K = 31
CONV_HIST = 32
CONV_GROUP_TOKENS = 16
CONV_TAP_BLOCK = 8
CONV_MXU_SLACK = 5

SUBLANES = 8
LANES = 128
MXU_COLS = 256
PERM_ROWS = 256

V7X_VMEM_BYTES = 64 * 1024 * 1024
VMEM_LIMIT_BYTES = 60000 * 1024


def _rms(x, g):
    return x * lax.rsqrt(jnp.mean(x * x, axis=-1, keepdims=True) + EPS) * g


def _silu(x):
    return x * jax.nn.sigmoid(x)


def _dot(a, b):
    return jnp.dot(a, b, preferred_element_type=F32)


def _dot_nt(a, b, precision=None):
    return lax.dot_general(a, b, (((1,), (1,)), ((), ())), preferred_element_type=F32, precision=precision)


def _const_spec(shape):
    nd = len(shape)
    return pl.BlockSpec(shape, lambda *_: (0,) * nd, pipeline_mode=pl.Buffered(1))


def _rope(x, cosf, sina, sinb):
    return x * cosf + pltpu.roll(x, HEAD_DIM - ROT_DIM // 2, axis=1) * sina + pltpu.roll(x, ROT_DIM // 2, axis=1) * sinb


def _even_kernel(x_ref, cos_ref, sina_ref, sinb_ref, pre_g_ref, w_in_ref, pool_w_ref, pool_s_ref, w_out_ref,
                 post_g_ref, o_ref, k_scr, vt_scr, kbar_scr, sel_scr, acc_scr, pcarry_scr, mix_scr):
    step = pl.program_id(1)
    tq = x_ref.shape[0]
    bs = MOBA_BLOCK
    nb = kbar_scr.shape[0]
    A = ATT_WIDTH
    first, second = slice(0, bs), slice(bs, 2 * bs)
    blk_a = 2 * step

    @pl.when(step == 0)
    def _():
        kbar_scr[...] = jnp.zeros_like(kbar_scr)
        pcarry_scr[...] = jnp.zeros_like(pcarry_scr)

    xb = x_ref[...]
    h = _rms(xb, pre_g_ref[...]).astype(BF16)
    q = _dot(h, w_in_ref[:, 0:A])
    k = _dot(h, w_in_ref[:, A:2 * A])
    v = _dot(h, w_in_ref[:, 2 * A:3 * A])
    ga = _dot(h, w_in_ref[:, 3 * A:4 * A])
    pu = _dot(h, w_in_ref[:, 4 * A:4 * A + POOL_WIDTH])
    gb = _dot(h, w_in_ref[:, 4 * A + POOL_WIDTH:4 * A + 2 * POOL_WIDTH])

    cosf, sina, sinb = cos_ref[...], sina_ref[...], sinb_ref[...]
    hs = [slice(hh * HEAD_DIM, (hh + 1) * HEAD_DIM) for hh in range(ATT_HEADS)]
    q_r = [_rope(q[:, s], cosf, sina, sinb) for s in hs]
    k_r = jnp.concatenate([_rope(k[:, s], cosf, sina, sinb) for s in hs], axis=1)
    k_b = k_r.astype(BF16)
    vt_b = v.T.astype(BF16)

    blk = lax.broadcasted_iota(jnp.int32, (nb, 1), 0)
    kbar = jnp.where(blk == blk_a, jnp.mean(k_r[first, :], axis=0, keepdims=True),
                     jnp.where(blk == blk_a + 1, jnp.mean(k_r[second, :], axis=0, keepdims=True), kbar_scr[...]))
    kbar_scr[...] = kbar
    k_scr[blk_a] = k_b[first, :]
    k_scr[blk_a + 1] = k_b[second, :]
    vt_scr[blk_a] = vt_b[:, first]
    vt_scr[blk_a + 1] = vt_b[:, second]

    qblk = blk_a + jnp.where(lax.broadcasted_iota(jnp.int32, (1, tq), 1) >= bs, 1, 0)
    past = blk < qblk
    for hh, s in enumerate(hs):
        g = _dot_nt(kbar[:, s], q_r[hh], precision=lax.Precision.HIGHEST)
        g = jnp.where(past, g, -jnp.inf)
        rank = jnp.zeros(g.shape, F32)
        for m in range(nb):
            gm = g[m:m + 1, :]
            beats = (gm > g) | ((gm == g) & (m < blk))
            rank = rank + jnp.where(beats, 1.0, 0.0)
        sel_scr[hh] = jnp.where(past & (rank < MOBA_TOPK), 1.0, 0.0)

    ext = jnp.concatenate([pcarry_scr[...], pu], axis=0)
    pcarry_scr[...] = pu[tq - POOL_CARRY:, :]
    tpos = step * tq + lax.broadcasted_iota(jnp.int32, (tq, 1), 0)
    for gi, w in enumerate(POOL_WINDOWS):
        cs = slice(gi * POOL_GROUP_W, (gi + 1) * POOL_GROUP_W)
        cum = ext[:, cs]
        d = 1
        while d < w:
            cum = cum + pltpu.roll(cum, d, axis=0)
            d *= 2
        cnt = jnp.minimum(tpos + 1, w).astype(F32)
        mg = cum[POOL_CARRY:, :] / cnt - pu[:, cs]
        yg = _dot(mg.astype(BF16), pool_w_ref[gi]) * pool_s_ref[:, cs] * _silu(gb[:, cs])
        mix_scr[:, A + gi * POOL_GROUP_W:A + (gi + 1) * POOL_GROUP_W] = yg.astype(BF16)

    qscale = math.log2(math.e) / math.sqrt(HEAD_DIM)
    qb = [(qh * qscale).astype(BF16) for qh in q_r]
    gate_a = [_silu(ga[:, s]) for s in hs]
    key_i = lax.broadcasted_iota(jnp.int32, (bs, bs), 0)
    qry_i = lax.broadcasted_iota(jnp.int32, (bs, bs), 1)
    causal = key_i <= qry_i
    nh = range(ATT_HEADS)

    sel_ab = [sel_scr[hh, pl.ds(blk_a, 1), second] > 0.5 for hh in nh]
    t_aa = [jnp.where(causal, _dot_nt(k_b[first, hs[hh]], qb[hh][first, :]), NEG) for hh in nh]
    t_ab = [jnp.where(sel_ab[hh], _dot_nt(k_b[first, hs[hh]], qb[hh][second, :]), NEG) for hh in nh]
    t_bb = [jnp.where(causal, _dot_nt(k_b[second, hs[hh]], qb[hh][second, :]), NEG) for hh in nh]
    m_a = [jnp.max(t_aa[hh], axis=0, keepdims=True) for hh in nh]
    m_b = [jnp.maximum(jnp.max(t_ab[hh], axis=0, keepdims=True), jnp.max(t_bb[hh], axis=0, keepdims=True)) for hh in nh]
    p_aa = [jnp.exp2(t_aa[hh] - m_a[hh]) for hh in nh]
    p_ab = [jnp.exp2(t_ab[hh] - m_b[hh]) for hh in nh]
    p_bb = [jnp.exp2(t_bb[hh] - m_b[hh]) for hh in nh]
    m0 = [jnp.concatenate([m_a[hh], m_b[hh]], axis=1) for hh in nh]
    l0 = [jnp.concatenate([jnp.sum(p_aa[hh], axis=0, keepdims=True),
                           jnp.sum(p_ab[hh], axis=0, keepdims=True) + jnp.sum(p_bb[hh], axis=0, keepdims=True)],
                          axis=1) for hh in nh]
    for hh in nh:
        acc_scr[hh, :, first] = _dot(vt_b[hs[hh], first], p_aa[hh].astype(BF16))
        acc_scr[hh, :, second] = _dot(vt_b[hs[hh], :],
                                      jnp.concatenate([p_ab[hh].astype(BF16), p_bb[hh].astype(BF16)], axis=0))

    def past_pair(jp, carry):
        m_run, l_run = carry
        j0 = 2 * jp
        ch = [(hh, j0 + d) for d in range(2) for hh in nh]
        st = [_dot_nt(k_scr[j, :, hs[hh]], qb[hh]) for hh, j in ch]
        st = [jnp.where(sel_scr[hh, pl.ds(j, 1), :] > 0.5, st[c], NEG) for c, (hh, j) in enumerate(ch)]
        cmax = [jnp.max(t, axis=0, keepdims=True) for t in st]
        m_new = [jnp.maximum(m_run[hh], jnp.maximum(cmax[hh], cmax[ATT_HEADS + hh])) for hh in nh]
        alpha = [jnp.exp2(m_run[hh] - m_new[hh]) for hh in nh]
        p = [jnp.exp2(st[c] - m_new[hh]) for c, (hh, j) in enumerate(ch)]
        csum = [jnp.sum(t, axis=0, keepdims=True) for t in p]
        l_new = [alpha[hh] * l_run[hh] + csum[hh] + csum[ATT_HEADS + hh] for hh in nh]
        pv = [_dot(jnp.concatenate([vt_scr[j0, hs[hh], :], vt_scr[j0 + 1, hs[hh], :]], axis=1),
                   jnp.concatenate([p[hh].astype(BF16), p[ATT_HEADS + hh].astype(BF16)], axis=0)) for hh in nh]
        for hh in nh:
            acc_scr[hh] = alpha[hh] * acc_scr[hh] + pv[hh]
        return tuple(m_new), tuple(l_new)

    _, l_f = lax.fori_loop(0, step, past_pair, (tuple(m0), tuple(l0)))
    for hh in nh:
        att = (acc_scr[hh] * (1.0 / l_f[hh])).T
        mix_scr[:, hs[hh]] = (att * gate_a[hh]).astype(BF16)

    y = _dot(mix_scr[...], w_out_ref[...])
    o_ref[...] = xb + _rms(y, post_g_ref[...])


def _rope_tables(seq):
    pos = jnp.arange(seq, dtype=F32)
    inv = jnp.power(ROPE_THETA, -jnp.arange(0, ROT_DIM, 2, dtype=F32) / ROT_DIM)
    ang = pos[:, None] * inv[None, :]
    cos, sin = jnp.cos(ang), jnp.sin(ang)
    half = ROT_DIM // 2
    zeros = jnp.zeros((seq, HEAD_DIM - ROT_DIM), F32)
    zh = jnp.zeros((seq, half), F32)
    cosf = jnp.concatenate([cos, cos, jnp.ones((seq, HEAD_DIM - ROT_DIM), F32)], axis=1)
    sina = jnp.concatenate([-sin, zh, zeros], axis=1)
    sinb = jnp.concatenate([zh, sin, zeros], axis=1)
    return cosf, sina, sinb


def _even_layer(x, pre_g, post_g, w_in, pool_w, pool_scale, w_out):
    b, s, d = x.shape
    bs = MOBA_BLOCK
    tq = 2 * bs
    nb = s // bs
    cosf, sina, sinb = _rope_tables(s)
    tab_spec = pl.BlockSpec((tq, HEAD_DIM), lambda bb, ii: (ii, 0))
    x_spec = pl.BlockSpec((None, tq, d), lambda bb, ii: (bb, ii, 0))
    return pl.pallas_call(
        _even_kernel,
        grid=(b, s // tq),
        in_specs=[x_spec, tab_spec, tab_spec, tab_spec,
                  _const_spec((1, d)), _const_spec(w_in.shape), _const_spec(pool_w.shape),
                  _const_spec((1, POOL_WIDTH)), _const_spec(w_out.shape), _const_spec((1, d))],
        out_specs=x_spec,
        out_shape=jax.ShapeDtypeStruct(x.shape, x.dtype),
        scratch_shapes=[
            pltpu.VMEM((nb, bs, ATT_WIDTH), BF16),
            pltpu.VMEM((nb, ATT_WIDTH, bs), BF16),
            pltpu.VMEM((nb, ATT_WIDTH), F32),
            pltpu.VMEM((ATT_HEADS, nb, tq), F32),
            pltpu.VMEM((ATT_HEADS, HEAD_DIM, tq), F32),
            pltpu.VMEM((POOL_CARRY, POOL_WIDTH), F32),
            pltpu.VMEM((tq, ATT_WIDTH + POOL_WIDTH), BF16),
        ],
        compiler_params=pltpu.CompilerParams(dimension_semantics=("arbitrary", "arbitrary"),
                                             vmem_limit_bytes=VMEM_LIMIT_BYTES),
        name="even_layer",
    )(x, cosf, sina, sinb, pre_g.reshape(1, d), w_in.astype(BF16), pool_w.astype(BF16),
      pool_scale.reshape(1, POOL_WIDTH), w_out.astype(BF16), post_g.reshape(1, d))


def _odd_kernel(zero_ref, x_ref, perm_ref, pre_g_ref, w_in_ref, bb_ref, cc_ref, lamr_ref, lami_ref, d_ref, glu_w_ref,
                glu_b_ref, dwb_ref, ln_g_ref, ln_b_ref, pw_ref, w_out_ref, post_g_ref, o_ref,
                st_re, st_im, h_scr, xs_scr, work_scr, keep_scr, mix_scr, mixbt_scr, *, nbatch, tl):
    step = pl.program_id(0)
    rows = tl * nbatch
    hist = CONV_HIST * nbatch
    W = S5_WIDTH
    hw = 2 * S5_HALF_LANES
    half_in = S5_HALF_GROUPS * S5_GROUP_IN
    mc = MXU_COLS
    scan_chunks = S5_HALF_LANES // S5_SCAN_LANES

    gext0 = 0
    conv0 = gext0 + hist + rows
    spare0 = conv0 + rows
    su0, gc0, gd0, bu0 = 0, rows, 2 * rows, 3 * rows
    gl0 = bu0 + 4 * scan_chunks * rows

    def region(r0):
        return slice(r0, r0 + rows)

    @pl.when(step == 0)
    def _():
        st_re[...] = jnp.zeros_like(st_re)
        st_im[...] = jnp.zeros_like(st_im)
        work_scr[gext0:gext0 + hist, :] = jnp.zeros((hist, W), F32)

    d_model = x_ref.shape[2]
    tok_per_perm = PERM_ROWS // nbatch
    xb = x_ref[...].reshape(rows, d_model)
    h_bt = _rms(xb, pre_g_ref[...]).astype(BF16).reshape(nbatch, tl, d_model)
    for u in range(tl // tok_per_perm):
        sub = h_bt[:, u * tok_per_perm:(u + 1) * tok_per_perm, :].reshape(PERM_ROWS, d_model)
        h_scr[u * PERM_ROWS:(u + 1) * PERM_ROWS, :] = _dot(perm_ref[0], sub).astype(BF16)

    def finish(res):
        work_scr[spare0:spare0 + SUBLANES, 0:LANES] = res[rows - SUBLANES:, res.shape[1] - LANES:]

    def project(r0, c, col):
        def run():
            res = _dot(h_scr[...], w_in_ref[:, col:col + mc])
            keep_scr[region(r0), c * mc:(c + 1) * mc] = res
            finish(res)
        return run

    def s5_in(hf, c):
        def run():
            su_b = keep_scr[region(su0), hf * half_in:(hf + 1) * half_in].astype(BF16)
            res = _dot(su_b, bb_ref[hf, :, c * S5_SCAN_LANES:(c + 1) * S5_SCAN_LANES])
            keep_scr[region(bu0 + (hf * 2 * scan_chunks + c) * rows), :] = res
            finish(res)
        return run

    glu_chunks = CONV_WIDTH // mc
    mxu_work = []
    for c in range(1, glu_chunks):
        mxu_work += [project(gl0 + (c - 1) * rows, 0, 2 * W + c * mc),
                     project(gl0 + (c - 1) * rows, 1, 2 * W + CONV_WIDTH + c * mc)]
    mxu_work += [project(su0, c, c * mc) for c in range(W // mc)]
    mxu_work += [s5_in(hf, c) for hf in range(2) for c in range(2 * scan_chunks)]
    mxu_work += [project(gc0, c, W + c * mc) for c in range(W // mc)]
    mxu_work += [project(gd0, c, 2 * W + 2 * CONV_WIDTH + c * mc) for c in range(CONV_WIDTH // mc)]

    ca = _dot(h_scr[...], w_in_ref[:, 2 * W:2 * W + mc])
    cb = _dot(h_scr[...], w_in_ref[:, 2 * W + CONV_WIDTH:2 * W + CONV_WIDTH + mc])
    work_scr[gext0 + hist:gext0 + hist + rows, 0:mc] = ca * jax.nn.sigmoid(cb)
    lead = CONV_HIST - (CONV_K - 1)
    tiles_per_tok = nbatch // SUBLANES
    ng = CONV_GROUP_TOKENS
    zrow = pl.multiple_of(zero_ref[0] * SUBLANES, SUBLANES)
    n_groups = (CONV_WIDTH // LANES) * tiles_per_tok * (tl // ng)
    n_work = len(mxu_work)
    done = 0
    for lt in range(CONV_WIDTH // LANES):
        ls = slice(lt * LANES, (lt + 1) * LANES)
        if lt * LANES % mc == 0 and lt > 0:
            c = lt * LANES // mc
            while len(mxu_work) > n_work - 2 * c:
                mxu_work.pop(0)()
            glu = keep_scr[region(gl0 + (c - 1) * rows), :]
            work_scr[gext0 + hist:gext0 + hist + rows, c * mc:(c + 1) * mc] = glu[:, 0:mc] * jax.nn.sigmoid(glu[:, mc:2 * mc])
        for par in range(tiles_per_tok):
            for t0 in range(0, tl, ng):
                acc = [jnp.zeros((SUBLANES, LANES), F32) for _ in range(ng)]
                for k0 in range(0, CONV_K, CONV_TAP_BLOCK):
                    kn = min(CONV_TAP_BLOCK, CONV_K - k0)
                    taps = [dwb_ref[k0 + kk, :, ls] for kk in range(kn)]
                    for u in range(ng + kn - 1):
                        r0 = gext0 + ((t0 + lead + k0 + u) * tiles_per_tok + par) * SUBLANES
                        xin = work_scr[pl.ds(r0 + zrow, SUBLANES), ls]
                        for m in range(max(0, u - kn + 1), min(ng, u + 1)):
                            acc[m] = acc[m] + taps[u - m] * xin
                for m in range(ng):
                    r0 = conv0 + ((t0 + m) * tiles_per_tok + par) * SUBLANES
                    work_scr[r0:r0 + SUBLANES, ls] = acc[m]
                done += 1
                while len(mxu_work) > n_work - (max(0, done - CONV_MXU_SLACK) * n_work) // (n_groups - CONV_MXU_SLACK):
                    mxu_work.pop(0)()
    while mxu_work:
        mxu_work.pop(0)()
    work_scr[gext0:gext0 + hist, :] = work_scr[gext0 + rows:gext0 + rows + hist, :]
    conv = work_scr[region(conv0), :]
    mu = jnp.mean(conv, axis=-1, keepdims=True)
    cen = conv - mu
    var = jnp.mean(cen * cen, axis=-1, keepdims=True)
    nrm = cen * lax.rsqrt(var + EPS) * ln_g_ref[...] + ln_b_ref[...]
    mix_scr[:, W:W + CONV_WIDTH] = (_dot(_silu(nrm).astype(BF16), pw_ref[...])
                                    * _silu(keep_scr[region(gd0), :])).astype(BF16)

    y_half = []
    for c in range(2 * scan_chunks):
        hf, sub = divmod(c, scan_chunks)
        cl = slice(c * S5_SCAN_LANES, (c + 1) * S5_SCAN_LANES)
        re = slice(hf * hw + sub * S5_SCAN_LANES, hf * hw + (sub + 1) * S5_SCAN_LANES)
        im = slice(re.start + S5_HALF_LANES, re.stop + S5_HALF_LANES)
        bu_re = bu0 + (hf * 2 * scan_chunks + sub) * rows
        bu_im = bu_re + scan_chunks * rows
        ar, ai = lamr_ref[:, cl], lami_ref[:, cl]
        sr, si = st_re[:, cl], st_im[:, cl]
        for t in range(tl):
            rs = slice(t * nbatch, (t + 1) * nbatch)
            br = keep_scr[bu_re + t * nbatch:bu_re + (t + 1) * nbatch, :]
            bi = keep_scr[bu_im + t * nbatch:bu_im + (t + 1) * nbatch, :]
            sr, si = ar * sr - ai * si + br, ar * si + ai * sr + bi
            xs_scr[rs, re] = sr.astype(BF16)
            xs_scr[rs, im] = si.astype(BF16)
        st_re[:, cl] = sr
        st_im[:, cl] = si
        if sub == scan_chunks - 1:
            y_half.append(_dot(xs_scr[:, hf * hw:(hf + 1) * hw], cc_ref[hf]))

    y = jnp.concatenate(y_half, axis=1) + d_ref[...] * keep_scr[region(su0), :]
    yy = _dot(y.astype(BF16), glu_w_ref[...]) + glu_b_ref[...]
    mix_scr[:, 0:W] = (yy[:, 0:W] * jax.nn.sigmoid(yy[:, W:2 * W]) * _silu(keep_scr[region(gc0), :])).astype(BF16)

    for u in range(tl // tok_per_perm):
        back = _dot(perm_ref[1], mix_scr[u * PERM_ROWS:(u + 1) * PERM_ROWS, :]).astype(BF16)
        mixbt_scr[:, u * tok_per_perm:(u + 1) * tok_per_perm, :] = back.reshape(nbatch, tok_per_perm, W + CONV_WIDTH)
    yo = _dot(mixbt_scr[...].reshape(rows, W + CONV_WIDTH), w_out_ref[...])
    o_ref[...] = (xb + _rms(yo, post_g_ref[...])).reshape(nbatch, tl, d_model)


def _s5_operands(lam_re, lam_im, log_dt, b_re, b_im, c_re, c_im, nbatch):
    lr, li = lam_re.astype(F32), lam_im.astype(F32)
    dt = jnp.exp(log_dt.astype(F32))[:, None]
    mag = jnp.exp(lr * dt)
    ab_re = mag * jnp.cos(li * dt)
    ab_im = mag * jnp.sin(li * dt)
    den = lr * lr + li * li
    num_re = ab_re - 1.0
    f_re = (num_re * lr + ab_im * li) / den
    f_im = (ab_im * lr - num_re * li) / den
    br, bi = b_re.astype(F32), b_im.astype(F32)
    bb_re = f_re[..., None] * br - f_im[..., None] * bi
    bb_im = f_re[..., None] * bi + f_im[..., None] * br
    hg, p, n_in = S5_HALF_GROUPS, S5_STATE, S5_GROUP_IN
    eye = jnp.eye(hg, dtype=F32)

    def pack_b(m):
        return jnp.einsum('hgpj,gk->hgjkp', m.reshape(2, hg, p, n_in), eye).reshape(2, hg * n_in, hg * p)

    def pack_c(m):
        return jnp.einsum('hgip,gk->hgpki', m.reshape(2, hg, n_in, p), eye).reshape(2, hg * p, hg * n_in)

    bb = jnp.concatenate([pack_b(bb_re), pack_b(bb_im)], axis=2).astype(BF16)
    cc = jnp.concatenate([pack_c(c_re.astype(F32)), -pack_c(c_im.astype(F32))], axis=1).astype(BF16)
    lanes = S5_GROUPS * S5_STATE
    lamr = jnp.broadcast_to(ab_re.reshape(1, lanes), (nbatch, lanes))
    lami = jnp.broadcast_to(ab_im.reshape(1, lanes), (nbatch, lanes))
    return bb, cc, lamr, lami


def _row_permutations(nbatch):
    tok = PERM_ROWS // nbatch
    dst = jnp.arange(PERM_ROWS)
    src = (dst % nbatch) * tok + dst // nbatch
    fwd = (src[:, None] == jnp.arange(PERM_ROWS)[None, :]).astype(BF16)
    return jnp.stack([fwd, fwd.T])


def _odd_layer(x, pre_g, post_g, w_in, lam_re, lam_im, log_dt, b_re, b_im, c_re, c_im, d_skip,
               glu_w, glu_b, dw, ln_g, ln_b, pw, w_out, tl=32):
    nbatch, seq, d = x.shape
    rows = tl * nbatch
    bb, cc, lamr, lami = _s5_operands(lam_re, lam_im, log_dt, b_re, b_im, c_re, c_im, nbatch)
    lanes = S5_GROUPS * S5_STATE
    x_spec = pl.BlockSpec((nbatch, tl, d), lambda ss: (0, ss, 0))
    ops = (x, _row_permutations(nbatch), pre_g.reshape(1, d), w_in.astype(BF16), bb, cc, lamr, lami,
           d_skip.reshape(1, S5_WIDTH),
           glu_w.astype(BF16), glu_b.reshape(1, 2 * S5_WIDTH),
           jnp.broadcast_to(dw.astype(F32)[:, None, :], (CONV_K, SUBLANES, CONV_WIDTH)), ln_g.reshape(1, CONV_WIDTH),
           ln_b.reshape(1, CONV_WIDTH), pw.astype(BF16), w_out.astype(BF16), post_g.reshape(1, d))
    return pl.pallas_call(
        functools.partial(_odd_kernel, nbatch=nbatch, tl=tl),
        grid=(seq // tl,),
        in_specs=[pl.BlockSpec(memory_space=pltpu.SMEM), x_spec] + [_const_spec(o.shape) for o in ops[1:]],
        out_specs=x_spec,
        out_shape=jax.ShapeDtypeStruct(x.shape, x.dtype),
        scratch_shapes=[
            pltpu.VMEM((nbatch, lanes), F32),
            pltpu.VMEM((nbatch, lanes), F32),
            pltpu.VMEM((rows, d), BF16),
            pltpu.VMEM((rows, 2 * lanes), BF16),
            pltpu.VMEM((CONV_HIST * nbatch + 2 * rows + SUBLANES, CONV_WIDTH), F32),
            pltpu.VMEM(((3 + 2 * lanes // S5_SCAN_LANES + CONV_WIDTH // MXU_COLS - 1) * rows, S5_WIDTH), F32),
            pltpu.VMEM((rows, S5_WIDTH + CONV_WIDTH), BF16),
            pltpu.VMEM((nbatch, tl, S5_WIDTH + CONV_WIDTH), BF16),
        ],
        compiler_params=pltpu.CompilerParams(dimension_semantics=("arbitrary",),
                                             vmem_limit_bytes=VMEM_LIMIT_BYTES),
        name="odd_layer",
    )(jnp.zeros((1,), jnp.int32), *ops)


def kernel(x, pre_norm_g, post_norm_g, e_w_in, e_pool_w, e_pool_scale, e_w_out, o_w_in, o_lam_re, o_lam_im,
           o_log_dt, o_b_re, o_b_im, o_c_re, o_c_im, o_d, o_glu_w, o_glu_b, o_dw, o_ln_g, o_ln_b, o_pw, o_w_out):
    b, s, d = x.shape
    for i in range(pre_norm_g.shape[0]):
        j = i // 2
        if i % 2 == 0:
            x = _even_layer(x, pre_norm_g[i], post_norm_g[i], e_w_in[j], e_pool_w[j], e_pool_scale[j], e_w_out[j])
        else:
            x = _odd_layer(x, pre_norm_g[i], post_norm_g[i], o_w_in[j], o_lam_re[j], o_lam_im[j],
                           o_log_dt[j], o_b_re[j], o_b_im[j], o_c_re[j], o_c_im[j], o_d[j], o_glu_w[j],
                           o_glu_b[j], o_dw[j], o_ln_g[j], o_ln_b[j], o_pw[j], o_w_out[j])
    return x
```

```python
import functools
import math

import jax
import jax.numpy as jnp
from jax import lax
from jax.experimental import pallas as pl
from jax.experimental.pallas import tpu as pltpu

F32 = jnp.float32
BF16 = jnp.bfloat16

D_MODEL = 1024
EPS = 1e-6
NEG = -1e30

ATT_WIDTH = 512
HEAD_DIM = 128
ATT_HEADS = 4
ROT_DIM = 32
ROPE_THETA = 500000.0
MOBA_BLOCK = 256
MOBA_TOPK = 3
POOL_WIDTH = 512
POOL_WINDOWS = (2, 4, 8, 16)
POOL_GROUP_W = 128
POOL_CARRY = 16

S5_WIDTH = 512
S5_GROUP_IN = 16
S5_GROUPS = 32
S5_STATE = 64
S5_HALF_GROUPS = 16
S5_HALF_LANES = S5_HALF_GROUPS * S5_STATE
S5_SCAN_LANES = 512
CONV_WIDTH = 512
CONV_K = 31
CONV_HIST = 32
CONV_GROUP_TOKENS = 16
CONV_TAP_BLOCK = 8
CONV_MXU_SLACK = 5

SUBLANES = 8
LANES = 128
MXU_COLS = 256
PERM_ROWS = 256

V7X_VMEM_BYTES = 64 * 1024 * 1024
VMEM_LIMIT_BYTES = 60000 * 1024


def _rms(x, g):
    return x * lax.rsqrt(jnp.mean(x * x, axis=-1, keepdims=True) + EPS) * g


def _silu(x):
    return x * jax.nn.sigmoid(x)


def _dot(a, b):
    return jnp.dot(a, b, preferred_element_type=F32)


def _dot_nt(a, b, precision=None):
    return lax.dot_general(a, b, (((1,), (1,)), ((), ())), preferred_element_type=F32, precision=precision)


def _const_spec(shape):
    nd = len(shape)
    return pl.BlockSpec(shape, lambda *_: (0,) * nd, pipeline_mode=pl.Buffered(1))


def _rope(x, cosf, sina, sinb):
    return x * cosf + pltpu.roll(x, HEAD_DIM - ROT_DIM // 2, axis=1) * sina + pltpu.roll(x, ROT_DIM // 2, axis=1) * sinb


def _even_kernel(x_ref, cos_ref, sina_ref, sinb_ref, pre_g_ref, w_in_ref, pool_w_ref, pool_s_ref, w_out_ref,
                 post_g_ref, o_ref, k_scr, vt_scr, kbar_scr, sel_scr, acc_scr, pcarry_scr, mix_scr):
    step = pl.program_id(1)
    tq = x_ref.shape[0]
    bs = MOBA_BLOCK
    nb = kbar_scr.shape[0]
    A = ATT_WIDTH
    first, second = slice(0, bs), slice(bs, 2 * bs)
    blk_a = 2 * step

    @pl.when(step == 0)
    def _():
        kbar_scr[...] = jnp.zeros_like(kbar_scr)
        pcarry_scr[...] = jnp.zeros_like(pcarry_scr)

    xb = x_ref[...]
    h = _rms(xb, pre_g_ref[...]).astype(BF16)
    q = _dot(h, w_in_ref[:, 0:A])
    k = _dot(h, w_in_ref[:, A:2 * A])
    v = _dot(h, w_in_ref[:, 2 * A:3 * A])
    ga = _dot(h, w_in_ref[:, 3 * A:4 * A])
    pu = _dot(h, w_in_ref[:, 4 * A:4 * A + POOL_WIDTH])
    gb = _dot(h, w_in_ref[:, 4 * A + POOL_WIDTH:4 * A + 2 * POOL_WIDTH])

    cosf, sina, sinb = cos_ref[...], sina_ref[...], sinb_ref[...]
    hs = [slice(hh * HEAD_DIM, (hh + 1) * HEAD_DIM) for hh in range(ATT_HEADS)]
    q_r = [_rope(q[:, s], cosf, sina, sinb) for s in hs]
    k_r = jnp.concatenate([_rope(k[:, s], cosf, sina, sinb) for s in hs], axis=1)
    k_b = k_r.astype(BF16)
    vt_b = v.T.astype(BF16)

    blk = lax.broadcasted_iota(jnp.int32, (nb, 1), 0)
    kbar = jnp.where(blk == blk_a, jnp.mean(k_r[first, :], axis=0, keepdims=True),
                     jnp.where(blk == blk_a + 1, jnp.mean(k_r[second, :], axis=0, keepdims=True), kbar_scr[...]))
    kbar_scr[...] = kbar
    k_scr[blk_a] = k_b[first, :]
    k_scr[blk_a + 1] = k_b[second, :]
    vt_scr[blk_a] = vt_b[:, first]
    vt_scr[blk_a + 1] = vt_b[:, second]

    qblk = blk_a + jnp.where(lax.broadcasted_iota(jnp.int32, (1, tq), 1) >= bs, 1, 0)
    past = blk < qblk
    for hh, s in enumerate(hs):
        g = _dot_nt(kbar[:, s], q_r[hh], precision=lax.Precision.HIGHEST)
        g = jnp.where(past, g, -jnp.inf)
        rank = jnp.zeros(g.shape, F32)
        for m in range(nb):
            gm = g[m:m + 1, :]
            beats = (gm > g) | ((gm == g) & (m < blk))
            rank = rank + jnp.where(beats, 1.0, 0.0)
        sel_scr[hh] = jnp.where(past & (rank < MOBA_TOPK), 1.0, 0.0)

    ext = jnp.concatenate([pcarry_scr[...], pu], axis=0)
    pcarry_scr[...] = pu[tq - POOL_CARRY:, :]
    tpos = step * tq + lax.broadcasted_iota(jnp.int32, (tq, 1), 0)
    for gi, w in enumerate(POOL_WINDOWS):
        cs = slice(gi * POOL_GROUP_W, (gi + 1) * POOL_GROUP_W)
        cum = ext[:, cs]
        d = 1
        while d < w:
            cum = cum + pltpu.roll(cum, d, axis=0)
            d *= 2
        cnt = jnp.minimum(tpos + 1, w).astype(F32)
        mg = cum[POOL_CARRY:, :] / cnt - pu[:, cs]
        yg = _dot(mg.astype(BF16), pool_w_ref[gi]) * pool_s_ref[:, cs] * _silu(gb[:, cs])
        mix_scr[:, A + gi * POOL_GROUP_W:A + (gi + 1) * POOL_GROUP_W] = yg.astype(BF16)

    qscale = math.log2(math.e) / math.sqrt(HEAD_DIM)
    qb = [(qh * qscale).astype(BF16) for qh in q_r]
    gate_a = [_silu(ga[:, s]) for s in hs]
    key_i = lax.broadcasted_iota(jnp.int32, (bs, bs), 0)
    qry_i = lax.broadcasted_iota(jnp.int32, (bs, bs), 1)
    causal = key_i <= qry_i
    nh = range(ATT_HEADS)

    sel_ab = [sel_scr[hh, pl.ds(blk_a, 1), second] > 0.5 for hh in nh]
    t_a = [_dot_nt(k_b[first, hs[hh]], qb[hh]) for hh in nh]
    t_aa = [jnp.where(causal, t_a[hh][:, first], NEG) for hh in nh]
    t_ab = [jnp.where(sel_ab[hh], t_a[hh][:, second], NEG) for hh in nh]
    t_bb = [jnp.where(causal, _dot_nt(k_b[second, hs[hh]], qb[hh][second, :]), NEG) for hh in nh]
    m_a = [jnp.max(t_aa[hh], axis=0, keepdims=True) for hh in nh]
    m_b = [jnp.maximum(jnp.max(t_ab[hh], axis=0, keepdims=True), jnp.max(t_bb[hh], axis=0, keepdims=True)) for hh in nh]
    p_aa = [jnp.exp2(t_aa[hh] - m_a[hh]) for hh in nh]
    p_ab = [jnp.exp2(t_ab[hh] - m_b[hh]) for hh in nh]
    p_bb = [jnp.exp2(t_bb[hh] - m_b[hh]) for hh in nh]
    m0 = [jnp.concatenate([m_a[hh], m_b[hh]], axis=1) for hh in nh]
    l0 = [jnp.concatenate([jnp.sum(p_aa[hh], axis=0, keepdims=True),
                           jnp.sum(p_ab[hh], axis=0, keepdims=True) + jnp.sum(p_bb[hh], axis=0, keepdims=True)],
                          axis=1) for hh in nh]
    for hh in nh:
        acc_scr[hh, :, first] = _dot(vt_b[hs[hh], first], p_aa[hh].astype(BF16))
        acc_scr[hh, :, second] = _dot(vt_b[hs[hh], :],
                                      jnp.concatenate([p_ab[hh].astype(BF16), p_bb[hh].astype(BF16)], axis=0))

    def past_pair(jp, carry):
        m_run, l_run = carry
        j0 = 2 * jp
        ch = [(hh, j0 + d) for d in range(2) for hh in nh]
        both = [_dot_nt(k_scr[pl.ds(j0, 2), :, hs[hh]].reshape(2 * bs, HEAD_DIM), qb[hh]) for hh in nh]
        st = [both[hh][d * bs:(d + 1) * bs, :] for d in range(2) for hh in nh]
        st = [jnp.where(sel_scr[hh, pl.ds(j, 1), :] > 0.5, st[c], NEG) for c, (hh, j) in enumerate(ch)]
        cmax = [jnp.max(t, axis=0, keepdims=True) for t in st]
        m_new = [jnp.maximum(m_run[hh], jnp.maximum(cmax[hh], cmax[ATT_HEADS + hh])) for hh in nh]
        alpha = [jnp.exp2(m_run[hh] - m_new[hh]) for hh in nh]
        p = [jnp.exp2(st[c] - m_new[hh]) for c, (hh, j) in enumerate(ch)]
        csum = [jnp.sum(t, axis=0, keepdims=True) for t in p]
        l_new = [alpha[hh] * l_run[hh] + csum[hh] + csum[ATT_HEADS + hh] for hh in nh]
        pv = [_dot(jnp.concatenate([vt_scr[j0, hs[hh], :], vt_scr[j0 + 1, hs[hh], :]], axis=1),
                   jnp.concatenate([p[hh].astype(BF16), p[ATT_HEADS + hh].astype(BF16)], axis=0)) for hh in nh]
        for hh in nh:
            acc_scr[hh] = alpha[hh] * acc_scr[hh] + pv[hh]
        return tuple(m_new), tuple(l_new)

    _, l_f = lax.fori_loop(0, step, past_pair, (tuple(m0), tuple(l0)))
    for hh in nh:
        att = (acc_scr[hh] * (1.0 / l_f[hh])).T
        mix_scr[:, hs[hh]] = (att * gate_a[hh]).astype(BF16)

    y = _dot(mix_scr[...], w_out_ref[...])
    o_ref[...] = xb + _rms(y, post_g_ref[...])


def _rope_tables(seq):
    pos = jnp.arange(seq, dtype=F32)
    inv = jnp.power(ROPE_THETA, -jnp.arange(0, ROT_DIM, 2, dtype=F32) / ROT_DIM)
    ang = pos[:, None] * inv[None, :]
    cos, sin = jnp.cos(ang), jnp.sin(ang)
    half = ROT_DIM // 2
    zeros = jnp.zeros((seq, HEAD_DIM - ROT_DIM), F32)
    zh = jnp.zeros((seq, half), F32)
    cosf = jnp.concatenate([cos, cos, jnp.ones((seq, HEAD_DIM - ROT_DIM), F32)], axis=1)
    sina = jnp.concatenate([-sin, zh, zeros], axis=1)
    sinb = jnp.concatenate([zh, sin, zeros], axis=1)
    return cosf, sina, sinb


def _even_layer(x, pre_g, post_g, w_in, pool_w, pool_scale, w_out):
    b, s, d = x.shape
    bs = MOBA_BLOCK
    tq = 2 * bs
    nb = s // bs
    cosf, sina, sinb = _rope_tables(s)
    tab_spec = pl.BlockSpec((tq, HEAD_DIM), lambda bb, ii: (ii, 0))
    x_spec = pl.BlockSpec((None, tq, d), lambda bb, ii: (bb, ii, 0))
    return pl.pallas_call(
        _even_kernel,
        grid=(b, s // tq),
        in_specs=[x_spec, tab_spec, tab_spec, tab_spec,
                  _const_spec((1, d)), _const_spec(w_in.shape), _const_spec(pool_w.shape),
                  _const_spec((1, POOL_WIDTH)), _const_spec(w_out.shape), _const_spec((1, d))],
        out_specs=x_spec,
        out_shape=jax.ShapeDtypeStruct(x.shape, x.dtype),
        scratch_shapes=[
            pltpu.VMEM((nb, bs, ATT_WIDTH), BF16),
            pltpu.VMEM((nb, ATT_WIDTH, bs), BF16),
            pltpu.VMEM((nb, ATT_WIDTH), F32),
            pltpu.VMEM((ATT_HEADS, nb, tq), F32),
            pltpu.VMEM((ATT_HEADS, HEAD_DIM, tq), F32),
            pltpu.VMEM((POOL_CARRY, POOL_WIDTH), F32),
            pltpu.VMEM((tq, ATT_WIDTH + POOL_WIDTH), BF16),
        ],
        compiler_params=pltpu.CompilerParams(dimension_semantics=("arbitrary", "arbitrary"),
                                             vmem_limit_bytes=VMEM_LIMIT_BYTES),
        name="even_layer",
    )(x, cosf, sina, sinb, pre_g.reshape(1, d), w_in.astype(BF16), pool_w.astype(BF16),
      pool_scale.reshape(1, POOL_WIDTH), w_out.astype(BF16), post_g.reshape(1, d))


def _odd_kernel(zero_ref, x_ref, perm_ref, pre_g_ref, w_in_ref, bb_ref, cc_ref, lamr_ref, lami_ref, d_ref, glu_w_ref,
                glu_b_ref, dwb_ref, ln_g_ref, ln_b_ref, pw_ref, w_out_ref, post_g_ref, o_ref,
                st_re, st_im, h_scr, xs_scr, work_scr, keep_scr, mix_scr, mixbt_scr, *, nbatch, tl):
    step = pl.program_id(0)
    rows = tl * nbatch
    hist = CONV_HIST * nbatch
    W = S5_WIDTH
    hw = 2 * S5_HALF_LANES
    half_in = S5_HALF_GROUPS * S5_GROUP_IN
    mc = MXU_COLS
    scan_chunks = S5_HALF_LANES // S5_SCAN_LANES

    gext0 = 0
    conv0 = gext0 + hist + rows
    spare0 = conv0 + rows
    su0, gc0, gd0, bu0 = 0, rows, 2 * rows, 3 * rows
    gl0 = bu0 + 4 * scan_chunks * rows

    def region(r0):
        return slice(r0, r0 + rows)

    @pl.when(step == 0)
    def _():
        st_re[...] = jnp.zeros_like(st_re)
        st_im[...] = jnp.zeros_like(st_im)
        work_scr[gext0:gext0 + hist, :] = jnp.zeros((hist, W), F32)

    d_model = x_ref.shape[2]
    tok_per_perm = PERM_ROWS // nbatch
    xb = x_ref[...].reshape(rows, d_model)
    h_bt = _rms(xb, pre_g_ref[...]).astype(BF16).reshape(nbatch, tl, d_model)
    for u in range(tl // tok_per_perm):
        sub = h_bt[:, u * tok_per_perm:(u + 1) * tok_per_perm, :].reshape(PERM_ROWS, d_model)
        h_scr[u * PERM_ROWS:(u + 1) * PERM_ROWS, :] = _dot(perm_ref[0], sub).astype(BF16)

    def finish(res):
        work_scr[spare0:spare0 + SUBLANES, 0:LANES] = res[rows - SUBLANES:, res.shape[1] - LANES:]

    def project(r0, c, col):
        def run():
            res = _dot(h_scr[...], w_in_ref[:, col:col + mc])
            keep_scr[region(r0), c * mc:(c + 1) * mc] = res
            finish(res)
        return run

    def s5_in(hf, c):
        def run():
            su_b = keep_scr[region(su0), hf * half_in:(hf + 1) * half_in].astype(BF16)
            res = _dot(su_b, bb_ref[hf, :, c * S5_SCAN_LANES:(c + 1) * S5_SCAN_LANES])
            keep_scr[region(bu0 + (hf * 2 * scan_chunks + c) * rows), :] = res
            finish(res)
        return run

    glu_chunks = CONV_WIDTH // mc
    mxu_work = []
    for c in range(1, glu_chunks):
        mxu_work += [project(gl0 + (c - 1) * rows, 0, 2 * W + c * mc),
                     project(gl0 + (c - 1) * rows, 1, 2 * W + CONV_WIDTH + c * mc)]
    mxu_work += [project(su0, c, c * mc) for c in range(W // mc)]
    mxu_work += [s5_in(hf, c) for hf in range(2) for c in range(2 * scan_chunks)]
    mxu_work += [project(gc0, c, W + c * mc) for c in range(W // mc)]
    mxu_work += [project(gd0, c, 2 * W + 2 * CONV_WIDTH + c * mc) for c in range(CONV_WIDTH // mc)]

    ca = _dot(h_scr[...], w_in_ref[:, 2 * W:2 * W + mc])
    cb = _dot(h_scr[...], w_in_ref[:, 2 * W + CONV_WIDTH:2 * W + CONV_WIDTH + mc])
    work_scr[gext0 + hist:gext0 + hist + rows, 0:mc] = ca * jax.nn.sigmoid(cb)
    lead = CONV_HIST - (CONV_K - 1)
    tiles_per_tok = nbatch // SUBLANES
    ng = CONV_GROUP_TOKENS
    zrow = pl.multiple_of(zero_ref[0] * SUBLANES, SUBLANES)
    n_groups = (CONV_WIDTH // LANES) * tiles_per_tok * (tl // ng)
    n_work = len(mxu_work)
    done = 0
    for lt in range(CONV_WIDTH // LANES):
        ls = slice(lt * LANES, (lt + 1) * LANES)
        if lt * LANES % mc == 0 and lt > 0:
            c = lt * LANES // mc
            while len(mxu_work) > n_work - 2 * c:
                mxu_work.pop(0)()
            glu = keep_scr[region(gl0 + (c - 1) * rows), :]
            work_scr[gext0 + hist:gext0 + hist + rows, c * mc:(c + 1) * mc] = glu[:, 0:mc] * jax.nn.sigmoid(glu[:, mc:2 * mc])
        for par in range(tiles_per_tok):
            for t0 in range(0, tl, ng):
                acc = [jnp.zeros((SUBLANES, LANES), F32) for _ in range(ng)]
                for k0 in range(0, CONV_K, CONV_TAP_BLOCK):
                    kn = min(CONV_TAP_BLOCK, CONV_K - k0)
                    taps = [dwb_ref[k0 + kk, :, ls] for kk in range(kn)]
                    for u in range(ng + kn - 1):
                        r0 = gext0 + ((t0 + lead + k0 + u) * tiles_per_tok + par) * SUBLANES
                        xin = work_scr[pl.ds(r0 + zrow, SUBLANES), ls]
                        for m in range(max(0, u - kn + 1), min(ng, u + 1)):
                            acc[m] = acc[m] + taps[u - m] * xin
                for m in range(ng):
                    r0 = conv0 + ((t0 + m) * tiles_per_tok + par) * SUBLANES
                    work_scr[r0:r0 + SUBLANES, ls] = acc[m]
                done += 1
                while len(mxu_work) > n_work - (max(0, done - CONV_MXU_SLACK) * n_work) // (n_groups - CONV_MXU_SLACK):
                    mxu_work.pop(0)()
    while mxu_work:
        mxu_work.pop(0)()
    work_scr[gext0:gext0 + hist, :] = work_scr[gext0 + rows:gext0 + rows + hist, :]
    conv = work_scr[region(conv0), :]
    mu = jnp.mean(conv, axis=-1, keepdims=True)
    cen = conv - mu
    var = jnp.mean(cen * cen, axis=-1, keepdims=True)
    nrm = cen * lax.rsqrt(var + EPS) * ln_g_ref[...] + ln_b_ref[...]
    mix_scr[:, W:W + CONV_WIDTH] = (_dot(_silu(nrm).astype(BF16), pw_ref[...])
                                    * _silu(keep_scr[region(gd0), :])).astype(BF16)

    y_half = []
    for c in range(2 * scan_chunks):
        hf, sub = divmod(c, scan_chunks)
        cl = slice(c * S5_SCAN_LANES, (c + 1) * S5_SCAN_LANES)
        re = slice(hf * hw + sub * S5_SCAN_LANES, hf * hw + (sub + 1) * S5_SCAN_LANES)
        im = slice(re.start + S5_HALF_LANES, re.stop + S5_HALF_LANES)
        bu_re = bu0 + (hf * 2 * scan_chunks + sub) * rows
        bu_im = bu_re + scan_chunks * rows
        ar, ai = lamr_ref[:, cl], lami_ref[:, cl]
        sr, si = st_re[:, cl], st_im[:, cl]
        for t in range(tl):
            rs = slice(t * nbatch, (t + 1) * nbatch)
            br = keep_scr[bu_re + t * nbatch:bu_re + (t + 1) * nbatch, :]
            bi = keep_scr[bu_im + t * nbatch:bu_im + (t + 1) * nbatch, :]
            sr, si = ar * sr - ai * si + br, ar * si + ai * sr + bi
            xs_scr[rs, re] = sr.astype(BF16)
            xs_scr[rs, im] = si.astype(BF16)
        st_re[:, cl] = sr
        st_im[:, cl] = si
        if sub == scan_chunks - 1:
            y_half.append(_dot(xs_scr[:, hf * hw:(hf + 1) * hw], cc_ref[hf]))

    y = jnp.concatenate(y_half, axis=1) + d_ref[...] * keep_scr[region(su0), :]
    yy = _dot(y.astype(BF16), glu_w_ref[...]) + glu_b_ref[...]
    mix_scr[:, 0:W] = (yy[:, 0:W] * jax.nn.sigmoid(yy[:, W:2 * W]) * _silu(keep_scr[region(gc0), :])).astype(BF16)

    for u in range(tl // tok_per_perm):
        back = _dot(perm_ref[1], mix_scr[u * PERM_ROWS:(u + 1) * PERM_ROWS, :]).astype(BF16)
        mixbt_scr[:, u * tok_per_perm:(u + 1) * tok_per_perm, :] = back.reshape(nbatch, tok_per_perm, W + CONV_WIDTH)
    yo = _dot(mixbt_scr[...].reshape(rows, W + CONV_WIDTH), w_out_ref[...])
    o_ref[...] = (xb + _rms(yo, post_g_ref[...])).reshape(nbatch, tl, d_model)


def _s5_operands(lam_re, lam_im, log_dt, b_re, b_im, c_re, c_im, nbatch):
    lr, li = lam_re.astype(F32), lam_im.astype(F32)
    dt = jnp.exp(log_dt.astype(F32))[:, None]
    mag = jnp.exp(lr * dt)
    ab_re = mag * jnp.cos(li * dt)
    ab_im = mag * jnp.sin(li * dt)
    den = lr * lr + li * li
    num_re = ab_re - 1.0
    f_re = (num_re * lr + ab_im * li) / den
    f_im = (ab_im * lr - num_re * li) / den
    br, bi = b_re.astype(F32), b_im.astype(F32)
    bb_re = f_re[..., None] * br - f_im[..., None] * bi
    bb_im = f_re[..., None] * bi + f_im[..., None] * br
    hg, p, n_in = S5_HALF_GROUPS, S5_STATE, S5_GROUP_IN
    eye = jnp.eye(hg, dtype=F32)

    def pack_b(m):
        return jnp.einsum('hgpj,gk->hgjkp', m.reshape(2, hg, p, n_in), eye).reshape(2, hg * n_in, hg * p)

    def pack_c(m):
        return jnp.einsum('hgip,gk->hgpki', m.reshape(2, hg, n_in, p), eye).reshape(2, hg * p, hg * n_in)

    bb = jnp.concatenate([pack_b(bb_re), pack_b(bb_im)], axis=2).astype(BF16)
    cc = jnp.concatenate([pack_c(c_re.astype(F32)), -pack_c(c_im.astype(F32))], axis=1).astype(BF16)
    lanes = S5_GROUPS * S5_STATE
    lamr = jnp.broadcast_to(ab_re.reshape(1, lanes), (nbatch, lanes))
    lami = jnp.broadcast_to(ab_im.reshape(1, lanes), (nbatch, lanes))
    return bb, cc, lamr, lami


def _row_permutations(nbatch):
    tok = PERM_ROWS // nbatch
    dst = jnp.arange(PERM_ROWS)
    src = (dst % nbatch) * tok + dst // nbatch
    fwd = (src[:, None] == jnp.arange(PERM_ROWS)[None, :]).astype(BF16)
    return jnp.stack([fwd, fwd.T])


def _odd_layer(x, pre_g, post_g, w_in, lam_re, lam_im, log_dt, b_re, b_im, c_re, c_im, d_skip,
               glu_w, glu_b, dw, ln_g, ln_b, pw, w_out, tl=32):
    nbatch, seq, d = x.shape
    rows = tl * nbatch
    bb, cc, lamr, lami = _s5_operands(lam_re, lam_im, log_dt, b_re, b_im, c_re, c_im, nbatch)
    lanes = S5_GROUPS * S5_STATE
    x_spec = pl.BlockSpec((nbatch, tl, d), lambda ss: (0, ss, 0))
    ops = (x, _row_permutations(nbatch), pre_g.reshape(1, d), w_in.astype(BF16), bb, cc, lamr, lami,
           d_skip.reshape(1, S5_WIDTH),
           glu_w.astype(BF16), glu_b.reshape(1, 2 * S5_WIDTH),
           jnp.broadcast_to(dw.astype(F32)[:, None, :], (CONV_K, SUBLANES, CONV_WIDTH)), ln_g.reshape(1, CONV_WIDTH),
           ln_b.reshape(1, CONV_WIDTH), pw.astype(BF16), w_out.astype(BF16), post_g.reshape(1, d))
    return pl.pallas_call(
        functools.partial(_odd_kernel, nbatch=nbatch, tl=tl),
        grid=(seq // tl,),
        in_specs=[pl.BlockSpec(memory_space=pltpu.SMEM), x_spec] + [_const_spec(o.shape) for o in ops[1:]],
        out_specs=x_spec,
        out_shape=jax.ShapeDtypeStruct(x.shape, x.dtype),
        scratch_shapes=[
            pltpu.VMEM((nbatch, lanes), F32),
            pltpu.VMEM((nbatch, lanes), F32),
            pltpu.VMEM((rows, d), BF16),
            pltpu.VMEM((rows, 2 * lanes), BF16),
            pltpu.VMEM((CONV_HIST * nbatch + 2 * rows + SUBLANES, CONV_WIDTH), F32),
            pltpu.VMEM(((3 + 2 * lanes // S5_SCAN_LANES + CONV_WIDTH // MXU_COLS - 1) * rows, S5_WIDTH), F32),
            pltpu.VMEM((rows, S5_WIDTH + CONV_WIDTH), BF16),
            pltpu.VMEM((nbatch, tl, S5_WIDTH + CONV_WIDTH), BF16),
        ],
        compiler_params=pltpu.CompilerParams(dimension_semantics=("arbitrary",),
                                             vmem_limit_bytes=VMEM_LIMIT_BYTES),
        name="odd_layer",
    )(jnp.zeros((1,), jnp.int32), *ops)


def kernel(x, pre_norm_g, post_norm_g, e_w_in, e_pool_w, e_pool_scale, e_w_out, o_w_in, o_lam_re, o_lam_im,
           o_log_dt, o_b_re, o_b_im, o_c_re, o_c_im, o_d, o_glu_w, o_glu_b, o_dw, o_ln_g, o_ln_b, o_pw, o_w_out):
    b, s, d = x.shape
    for i in range(pre_norm_g.shape[0]):
        j = i // 2
        if i % 2 == 0:
            x = _even_layer(x, pre_norm_g[i], post_norm_g[i], e_w_in[j], e_pool_w[j], e_pool_scale[j], e_w_out[j])
        else:
            x = _odd_layer(x, pre_norm_g[i], post_norm_g[i], o_w_in[j], o_lam_re[j], o_lam_im[j],
                           o_log_dt[j], o_b_re[j], o_b_im[j], o_c_re[j], o_c_im[j], o_d[j], o_glu_w[j],
                           o_glu_b[j], o_dw[j], o_ln_g[j], o_ln_b[j], o_pw[j], o_w_out[j])
    return x
```

```python
import functools
import math

import jax
import jax.numpy as jnp
from jax import lax
from jax.experimental import pallas as pl
from jax.experimental.pallas import tpu as pltpu

F32 = jnp.float32
BF16 = jnp.bfloat16

D_MODEL = 1024
EPS = 1e-6
NEG = -1e30

ATT_WIDTH = 512
HEAD_DIM = 128
ATT_HEADS = 4
ROT_DIM = 32
ROPE_THETA = 500000.0
MOBA_BLOCK = 256
MOBA_TOPK = 3
POOL_WIDTH = 512
POOL_WINDOWS = (2, 4, 8, 16)
POOL_GROUP_W = 128
POOL_CARRY = 16

S5_WIDTH = 512
S5_GROUP_IN = 16
S5_GROUPS = 32
S5_STATE = 64
S5_HALF_GROUPS = 16
S5_HALF_LANES = S5_HALF_GROUPS * S5_STATE
S5_SCAN_LANES = 512
CONV_WIDTH = 512
CONV_K = 31
CONV_HIST = 32
CONV_GROUP_TOKENS = 16
CONV_TAP_BLOCK = 8
CONV_MXU_SLACK = 5

SUBLANES = 8
LANES = 128
MXU_COLS = 256
PERM_ROWS = 256

V7X_VMEM_BYTES = 64 * 1024 * 1024
VMEM_LIMIT_BYTES = 60000 * 1024


def _rms(x, g):
    return x * lax.rsqrt(jnp.mean(x * x, axis=-1, keepdims=True) + EPS) * g


def _silu(x):
    return x * jax.nn.sigmoid(x)


def _dot(a, b):
    return jnp.dot(a, b, preferred_element_type=F32)


def _dot_nt(a, b, precision=None):
    return lax.dot_general(a, b, (((1,), (1,)), ((), ())), preferred_element_type=F32, precision=precision)


def _const_spec(shape):
    nd = len(shape)
    return pl.BlockSpec(shape, lambda *_: (0,) * nd, pipeline_mode=pl.Buffered(1))


def _rope(x, cosf, sina, sinb):
    return x * cosf + pltpu.roll(x, HEAD_DIM - ROT_DIM // 2, axis=1) * sina + pltpu.roll(x, ROT_DIM // 2, axis=1) * sinb


def _even_kernel(x_ref, cos_ref, sina_ref, sinb_ref, pre_g_ref, w_in_ref, pool_w_ref, pool_s_ref, w_out_ref,
                 post_g_ref, o_ref, k_scr, vt_scr, kbar_scr, sel_scr, acc_scr, pcarry_scr, mix_scr):
    step = pl.program_id(1)
    tq = x_ref.shape[0]
    bs = MOBA_BLOCK
    nb = kbar_scr.shape[0]
    A = ATT_WIDTH
    first, second = slice(0, bs), slice(bs, 2 * bs)
    blk_a = 2 * step

    @pl.when(step == 0)
    def _():
        kbar_scr[...] = jnp.zeros_like(kbar_scr)
        pcarry_scr[...] = jnp.zeros_like(pcarry_scr)

    xb = x_ref[...]
    h = _rms(xb, pre_g_ref[...]).astype(BF16)
    q = _dot(h, w_in_ref[:, 0:A])
    k = _dot(h, w_in_ref[:, A:2 * A])
    v = _dot(h, w_in_ref[:, 2 * A:3 * A])
    ga = _dot(h, w_in_ref[:, 3 * A:4 * A])
    pu = _dot(h, w_in_ref[:, 4 * A:4 * A + POOL_WIDTH])
    gb = _dot(h, w_in_ref[:, 4 * A + POOL_WIDTH:4 * A + 2 * POOL_WIDTH])

    cosf, sina, sinb = cos_ref[...], sina_ref[...], sinb_ref[...]
    hs = [slice(hh * HEAD_DIM, (hh + 1) * HEAD_DIM) for hh in range(ATT_HEADS)]
    q_r = [_rope(q[:, s], cosf, sina, sinb) for s in hs]
    k_r = jnp.concatenate([_rope(k[:, s], cosf, sina, sinb) for s in hs], axis=1)
    k_b = k_r.astype(BF16)
    vt_b = v.T.astype(BF16)

    blk = lax.broadcasted_iota(jnp.int32, (nb, 1), 0)
    kbar = jnp.where(blk == blk_a, jnp.mean(k_r[first, :], axis=0, keepdims=True),
                     jnp.where(blk == blk_a + 1, jnp.mean(k_r[second, :], axis=0, keepdims=True), kbar_scr[...]))
    kbar_scr[...] = kbar
    k_scr[blk_a] = k_b[first, :]
    k_scr[blk_a + 1] = k_b[second, :]
    vt_scr[blk_a] = vt_b[:, first]
    vt_scr[blk_a + 1] = vt_b[:, second]

    qblk = blk_a + jnp.where(lax.broadcasted_iota(jnp.int32, (1, tq), 1) >= bs, 1, 0)
    past = blk < qblk
    for hh, s in enumerate(hs):
        g = _dot_nt(kbar[:, s], q_r[hh], precision=lax.Precision.HIGHEST)
        g = jnp.where(past, g, -jnp.inf)
        rank = jnp.zeros(g.shape, F32)
        for m in range(nb):
            gm = g[m:m + 1, :]
            beats = (gm > g) | ((gm == g) & (m < blk))
            rank = rank + jnp.where(beats, 1.0, 0.0)
        sel_scr[hh] = jnp.where(past & (rank < MOBA_TOPK), 1.0, 0.0)

    ext = jnp.concatenate([pcarry_scr[...], pu], axis=0)
    pcarry_scr[...] = pu[tq - POOL_CARRY:, :]
    tpos = step * tq + lax.broadcasted_iota(jnp.int32, (tq, 1), 0)
    for gi, w in enumerate(POOL_WINDOWS):
        cs = slice(gi * POOL_GROUP_W, (gi + 1) * POOL_GROUP_W)
        cum = ext[:, cs]
        d = 1
        while d < w:
            cum = cum + pltpu.roll(cum, d, axis=0)
            d *= 2
        cnt = jnp.minimum(tpos + 1, w).astype(F32)
        mg = cum[POOL_CARRY:, :] / cnt - pu[:, cs]
        yg = _dot(mg.astype(BF16), pool_w_ref[gi]) * pool_s_ref[:, cs] * _silu(gb[:, cs])
        mix_scr[:, A + gi * POOL_GROUP_W:A + (gi + 1) * POOL_GROUP_W] = yg.astype(BF16)

    qscale = math.log2(math.e) / math.sqrt(HEAD_DIM)
    qb = [(qh * qscale).astype(BF16) for qh in q_r]
    gate_a = [_silu(ga[:, s]) for s in hs]
    key_i = lax.broadcasted_iota(jnp.int32, (bs, bs), 0)
    qry_i = lax.broadcasted_iota(jnp.int32, (bs, bs), 1)
    causal = key_i <= qry_i
    nh = range(ATT_HEADS)

    sel_ab = [sel_scr[hh, pl.ds(blk_a, 1), second] > 0.5 for hh in nh]
    t_a = [_dot_nt(k_b[first, hs[hh]], qb[hh]) for hh in nh]
    t_aa = [jnp.where(causal, t_a[hh][:, first], NEG) for hh in nh]
    t_ab = [jnp.where(sel_ab[hh], t_a[hh][:, second], NEG) for hh in nh]
    t_bb = [jnp.where(causal, _dot_nt(k_b[second, hs[hh]], qb[hh][second, :]), NEG) for hh in nh]
    m_a = [jnp.max(t_aa[hh], axis=0, keepdims=True) for hh in nh]
    m_b = [jnp.maximum(jnp.max(t_ab[hh], axis=0, keepdims=True), jnp.max(t_bb[hh], axis=0, keepdims=True)) for hh in nh]
    p_aa = [jnp.exp2(t_aa[hh] - m_a[hh]) for hh in nh]
    p_ab = [jnp.exp2(t_ab[hh] - m_b[hh]) for hh in nh]
    p_bb = [jnp.exp2(t_bb[hh] - m_b[hh]) for hh in nh]
    m0 = [jnp.concatenate([m_a[hh], m_b[hh]], axis=1) for hh in nh]
    l0 = [jnp.concatenate([jnp.sum(p_aa[hh], axis=0, keepdims=True),
                           jnp.sum(p_ab[hh], axis=0, keepdims=True) + jnp.sum(p_bb[hh], axis=0, keepdims=True)],
                          axis=1) for hh in nh]
    for hh in nh:
        acc_scr[hh, :, first] = _dot(vt_b[hs[hh], first], p_aa[hh].astype(BF16))
        acc_scr[hh, :, second] = _dot(vt_b[hs[hh], :],
                                      jnp.concatenate([p_ab[hh].astype(BF16), p_bb[hh].astype(BF16)], axis=0))

    def past_pair(jp, carry):
        m_run, l_run = carry
        j0 = 2 * jp
        ch = [(hh, j0 + d) for d in range(2) for hh in nh]
        both = [_dot_nt(k_scr[pl.ds(j0, 2), :, hs[hh]].reshape(2 * bs, HEAD_DIM), qb[hh]) for hh in nh]
        st = [both[hh][d * bs:(d + 1) * bs, :] for d in range(2) for hh in nh]
        st = [jnp.where(sel_scr[hh, pl.ds(j, 1), :] > 0.5, st[c], NEG) for c, (hh, j) in enumerate(ch)]
        cmax = [jnp.max(t, axis=0, keepdims=True) for t in st]
        m_new = [jnp.maximum(m_run[hh], jnp.maximum(cmax[hh], cmax[ATT_HEADS + hh])) for hh in nh]
        alpha = [jnp.exp2(m_run[hh] - m_new[hh]) for hh in nh]
        p = [jnp.exp2(st[c] - m_new[hh]) for c, (hh, j) in enumerate(ch)]
        csum = [jnp.sum(t, axis=0, keepdims=True) for t in p]
        l_new = [alpha[hh] * l_run[hh] + csum[hh] + csum[ATT_HEADS + hh] for hh in nh]
        pv = [_dot(jnp.concatenate([vt_scr[j0, hs[hh], :], vt_scr[j0 + 1, hs[hh], :]], axis=1),
                   jnp.concatenate([p[hh].astype(BF16), p[ATT_HEADS + hh].astype(BF16)], axis=0)) for hh in nh]
        for hh in nh:
            acc_scr[hh] = alpha[hh] * acc_scr[hh] + pv[hh]
        return tuple(m_new), tuple(l_new)

    _, l_f = lax.fori_loop(0, step, past_pair, (tuple(m0), tuple(l0)))
    for hh in nh:
        att = (acc_scr[hh] * (1.0 / l_f[hh])).T
        mix_scr[:, hs[hh]] = (att * gate_a[hh]).astype(BF16)

    y = _dot(mix_scr[...], w_out_ref[...])
    o_ref[...] = xb + _rms(y, post_g_ref[...])


def _rope_tables(seq):
    pos = jnp.arange(seq, dtype=F32)
    inv = jnp.power(ROPE_THETA, -jnp.arange(0, ROT_DIM, 2, dtype=F32) / ROT_DIM)
    ang = pos[:, None] * inv[None, :]
    cos, sin = jnp.cos(ang), jnp.sin(ang)
    half = ROT_DIM // 2
    zeros = jnp.zeros((seq, HEAD_DIM - ROT_DIM), F32)
    zh = jnp.zeros((seq, half), F32)
    cosf = jnp.concatenate([cos, cos, jnp.ones((seq, HEAD_DIM - ROT_DIM), F32)], axis=1)
    sina = jnp.concatenate([-sin, zh, zeros], axis=1)
    sinb = jnp.concatenate([zh, sin, zeros], axis=1)
    return cosf, sina, sinb


def _even_layer(x, pre_g, post_g, w_in, pool_w, pool_scale, w_out):
    b, s, d = x.shape
    bs = MOBA_BLOCK
    tq = 2 * bs
    nb = s // bs
    cosf, sina, sinb = _rope_tables(s)
    tab_spec = pl.BlockSpec((tq, HEAD_DIM), lambda bb, ii: (ii, 0))
    x_spec = pl.BlockSpec((None, tq, d), lambda bb, ii: (bb, ii, 0))
    return pl.pallas_call(
        _even_kernel,
        grid=(b, s // tq),
        in_specs=[x_spec, tab_spec, tab_spec, tab_spec,
                  _const_spec((1, d)), _const_spec(w_in.shape), _const_spec(pool_w.shape),
                  _const_spec((1, POOL_WIDTH)), _const_spec(w_out.shape), _const_spec((1, d))],
        out_specs=x_spec,
        out_shape=jax.ShapeDtypeStruct(x.shape, x.dtype),
        scratch_shapes=[
            pltpu.VMEM((nb, bs, ATT_WIDTH), BF16),
            pltpu.VMEM((nb, ATT_WIDTH, bs), BF16),
            pltpu.VMEM((nb, ATT_WIDTH), F32),
            pltpu.VMEM((ATT_HEADS, nb, tq), F32),
            pltpu.VMEM((ATT_HEADS, HEAD_DIM, tq), F32),
            pltpu.VMEM((POOL_CARRY, POOL_WIDTH), F32),
            pltpu.VMEM((tq, ATT_WIDTH + POOL_WIDTH), BF16),
        ],
        compiler_params=pltpu.CompilerParams(dimension_semantics=("arbitrary", "arbitrary"),
                                             vmem_limit_bytes=VMEM_LIMIT_BYTES),
        name="even_layer",
    )(x, cosf, sina, sinb, pre_g.reshape(1, d), w_in.astype(BF16), pool_w.astype(BF16),
      pool_scale.reshape(1, POOL_WIDTH), w_out.astype(BF16), post_g.reshape(1, d))


def _odd_kernel(zero_ref, x_ref, xprev_ref, perm_ref, pre_g_ref, w_in_ref, bb_ref, cc_ref, lamr_ref, lami_ref, d_ref,
                glu_w_ref, glu_b_ref, dwb_ref, ln_g_ref, ln_b_ref, pw_ref, w_out_ref, post_g_ref, o_ref,
                st_re, st_im, h_scr, xs_scr, work_scr, keep_scr, mix_scr, mixbt_scr, yo_scr, *, nbatch, tl):
    step = pl.program_id(0)
    rows = tl * nbatch
    hist = CONV_HIST * nbatch
    W = S5_WIDTH
    hw = 2 * S5_HALF_LANES
    half_in = S5_HALF_GROUPS * S5_GROUP_IN
    mc = MXU_COLS
    scan_chunks = S5_HALF_LANES // S5_SCAN_LANES

    gext0 = 0
    conv0 = gext0 + hist + rows
    spare0 = conv0 + rows
    su0, gc0, gd0, bu0 = 0, rows, 2 * rows, 3 * rows
    gl0 = bu0 + 4 * scan_chunks * rows

    def region(r0):
        return slice(r0, r0 + rows)

    @pl.when(step == 0)
    def _():
        st_re[...] = jnp.zeros_like(st_re)
        st_im[...] = jnp.zeros_like(st_im)
        work_scr[gext0:gext0 + hist, :] = jnp.zeros((hist, W), F32)
        mixbt_scr[...] = jnp.zeros_like(mixbt_scr)

    d_model = x_ref.shape[2]
    mix_prev = mixbt_scr[...].reshape(rows, W + CONV_WIDTH)

    tok_per_perm = PERM_ROWS // nbatch
    xb = x_ref[...].reshape(rows, d_model)
    h_bt = _rms(xb, pre_g_ref[...]).astype(BF16).reshape(nbatch, tl, d_model)
    for u in range(tl // tok_per_perm):
        sub = h_bt[:, u * tok_per_perm:(u + 1) * tok_per_perm, :].reshape(PERM_ROWS, d_model)
        h_scr[u * PERM_ROWS:(u + 1) * PERM_ROWS, :] = _dot(perm_ref[0], sub).astype(BF16)

    def finish(res):
        work_scr[spare0:spare0 + SUBLANES, 0:LANES] = res[rows - SUBLANES:, res.shape[1] - LANES:]

    def project(r0, c, col):
        def run():
            res = _dot(h_scr[...], w_in_ref[:, col:col + mc])
            keep_scr[region(r0), c * mc:(c + 1) * mc] = res
            finish(res)
        return run

    def s5_in(hf, c):
        def run():
            su_b = keep_scr[region(su0), hf * half_in:(hf + 1) * half_in].astype(BF16)
            res = _dot(su_b, bb_ref[hf, :, c * S5_SCAN_LANES:(c + 1) * S5_SCAN_LANES])
            keep_scr[region(bu0 + (hf * 2 * scan_chunks + c) * rows), :] = res
            finish(res)
        return run

    glu_chunks = CONV_WIDTH // mc
    mxu_work = []
    for c in range(1, glu_chunks):
        mxu_work += [project(gl0 + (c - 1) * rows, 0, 2 * W + c * mc),
                     project(gl0 + (c - 1) * rows, 1, 2 * W + CONV_WIDTH + c * mc)]
    mxu_work += [project(su0, c, c * mc) for c in range(W // mc)]
    mxu_work += [s5_in(hf, c) for hf in range(2) for c in range(2 * scan_chunks)]
    mxu_work += [project(gc0, c, W + c * mc) for c in range(W // mc)]
    mxu_work += [project(gd0, c, 2 * W + 2 * CONV_WIDTH + c * mc) for c in range(CONV_WIDTH // mc)]

    def finish_prev(c):
        def run():
            res = _dot(mix_prev, w_out_ref[:, c * mc:(c + 1) * mc])
            yo_scr[:, c * mc:(c + 1) * mc] = res
            finish(res)
        return run

    mxu_work += [finish_prev(c) for c in range(d_model // mc)]

    ca = _dot(h_scr[...], w_in_ref[:, 2 * W:2 * W + mc])
    cb = _dot(h_scr[...], w_in_ref[:, 2 * W + CONV_WIDTH:2 * W + CONV_WIDTH + mc])
    work_scr[gext0 + hist:gext0 + hist + rows, 0:mc] = ca * jax.nn.sigmoid(cb)
    lead = CONV_HIST - (CONV_K - 1)
    tiles_per_tok = nbatch // SUBLANES
    ng = CONV_GROUP_TOKENS
    zrow = pl.multiple_of(zero_ref[0] * SUBLANES, SUBLANES)
    n_groups = (CONV_WIDTH // LANES) * tiles_per_tok * (tl // ng)
    n_work = len(mxu_work)
    done = 0
    for lt in range(CONV_WIDTH // LANES):
        ls = slice(lt * LANES, (lt + 1) * LANES)
        if lt * LANES % mc == 0 and lt > 0:
            c = lt * LANES // mc
            while len(mxu_work) > n_work - 2 * c:
                mxu_work.pop(0)()
            glu = keep_scr[region(gl0 + (c - 1) * rows), :]
            work_scr[gext0 + hist:gext0 + hist + rows, c * mc:(c + 1) * mc] = glu[:, 0:mc] * jax.nn.sigmoid(glu[:, mc:2 * mc])
        for par in range(tiles_per_tok):
            for t0 in range(0, tl, ng):
                acc = [jnp.zeros((SUBLANES, LANES), F32) for _ in range(ng)]
                for k0 in range(0, CONV_K, CONV_TAP_BLOCK):
                    kn = min(CONV_TAP_BLOCK, CONV_K - k0)
                    taps = [dwb_ref[k0 + kk, :, ls] for kk in range(kn)]
                    for u in range(ng + kn - 1):
                        r0 = gext0 + ((t0 + lead + k0 + u) * tiles_per_tok + par) * SUBLANES
                        xin = work_scr[pl.ds(r0 + zrow, SUBLANES), ls]
                        for m in range(max(0, u - kn + 1), min(ng, u + 1)):
                            acc[m] = acc[m] + taps[u - m] * xin
                for m in range(ng):
                    r0 = conv0 + ((t0 + m) * tiles_per_tok + par) * SUBLANES
                    work_scr[r0:r0 + SUBLANES, ls] = acc[m]
                done += 1
                while len(mxu_work) > n_work - (max(0, done - CONV_MXU_SLACK) * n_work) // (n_groups - CONV_MXU_SLACK):
                    mxu_work.pop(0)()
    while mxu_work:
        mxu_work.pop(0)()
    work_scr[gext0:gext0 + hist, :] = work_scr[gext0 + rows:gext0 + rows + hist, :]
    conv = work_scr[region(conv0), :]
    mu = jnp.mean(conv, axis=-1, keepdims=True)
    cen = conv - mu
    var = jnp.mean(cen * cen, axis=-1, keepdims=True)
    nrm = cen * lax.rsqrt(var + EPS) * ln_g_ref[...] + ln_b_ref[...]
    mix_scr[:, W:W + CONV_WIDTH] = (_dot(_silu(nrm).astype(BF16), pw_ref[...])
                                    * _silu(keep_scr[region(gd0), :])).astype(BF16)

    y_half = []
    for c in range(2 * scan_chunks):
        hf, sub = divmod(c, scan_chunks)
        cl = slice(c * S5_SCAN_LANES, (c + 1) * S5_SCAN_LANES)
        re = slice(hf * hw + sub * S5_SCAN_LANES, hf * hw + (sub + 1) * S5_SCAN_LANES)
        im = slice(re.start + S5_HALF_LANES, re.stop + S5_HALF_LANES)
        bu_re = bu0 + (hf * 2 * scan_chunks + sub) * rows
        bu_im = bu_re + scan_chunks * rows
        ar, ai = lamr_ref[:, cl], lami_ref[:, cl]
        sr, si = st_re[:, cl], st_im[:, cl]
        for t in range(tl):
            rs = slice(t * nbatch, (t + 1) * nbatch)
            br = keep_scr[bu_re + t * nbatch:bu_re + (t + 1) * nbatch, :]
            bi = keep_scr[bu_im + t * nbatch:bu_im + (t + 1) * nbatch, :]
            sr, si = ar * sr - ai * si + br, ar * si + ai * sr + bi
            xs_scr[rs, re] = sr.astype(BF16)
            xs_scr[rs, im] = si.astype(BF16)
        st_re[:, cl] = sr
        st_im[:, cl] = si
        if sub == scan_chunks - 1:
            y_half.append(_dot(xs_scr[:, hf * hw:(hf + 1) * hw], cc_ref[hf]))

    o_ref[...] = (xprev_ref[...].reshape(rows, d_model) + _rms(yo_scr[...], post_g_ref[...])).reshape(nbatch, tl, d_model)

    y = jnp.concatenate(y_half, axis=1) + d_ref[...] * keep_scr[region(su0), :]
    yy = _dot(y.astype(BF16), glu_w_ref[...]) + glu_b_ref[...]
    mix_scr[:, 0:W] = (yy[:, 0:W] * jax.nn.sigmoid(yy[:, W:2 * W]) * _silu(keep_scr[region(gc0), :])).astype(BF16)

    for u in range(tl // tok_per_perm):
        back = _dot(perm_ref[1], mix_scr[u * PERM_ROWS:(u + 1) * PERM_ROWS, :]).astype(BF16)
        mixbt_scr[:, u * tok_per_perm:(u + 1) * tok_per_perm, :] = back.reshape(nbatch, tok_per_perm, W + CONV_WIDTH)


def _s5_operands(lam_re, lam_im, log_dt, b_re, b_im, c_re, c_im, nbatch):
    lr, li = lam_re.astype(F32), lam_im.astype(F32)
    dt = jnp.exp(log_dt.astype(F32))[:, None]
    mag = jnp.exp(lr * dt)
    ab_re = mag * jnp.cos(li * dt)
    ab_im = mag * jnp.sin(li * dt)
    den = lr * lr + li * li
    num_re = ab_re - 1.0
    f_re = (num_re * lr + ab_im * li) / den
    f_im = (ab_im * lr - num_re * li) / den
    br, bi = b_re.astype(F32), b_im.astype(F32)
    bb_re = f_re[..., None] * br - f_im[..., None] * bi
    bb_im = f_re[..., None] * bi + f_im[..., None] * br
    hg, p, n_in = S5_HALF_GROUPS, S5_STATE, S5_GROUP_IN
    eye = jnp.eye(hg, dtype=F32)

    def pack_b(m):
        return jnp.einsum('hgpj,gk->hgjkp', m.reshape(2, hg, p, n_in), eye).reshape(2, hg * n_in, hg * p)

    def pack_c(m):
        return jnp.einsum('hgip,gk->hgpki', m.reshape(2, hg, n_in, p), eye).reshape(2, hg * p, hg * n_in)

    bb = jnp.concatenate([pack_b(bb_re), pack_b(bb_im)], axis=2).astype(BF16)
    cc = jnp.concatenate([pack_c(c_re.astype(F32)), -pack_c(c_im.astype(F32))], axis=1).astype(BF16)
    lanes = S5_GROUPS * S5_STATE
    lamr = jnp.broadcast_to(ab_re.reshape(1, lanes), (nbatch, lanes))
    lami = jnp.broadcast_to(ab_im.reshape(1, lanes), (nbatch, lanes))
    return bb, cc, lamr, lami


def _row_permutations(nbatch):
    tok = PERM_ROWS // nbatch
    dst = jnp.arange(PERM_ROWS)
    src = (dst % nbatch) * tok + dst // nbatch
    fwd = (src[:, None] == jnp.arange(PERM_ROWS)[None, :]).astype(BF16)
    return jnp.stack([fwd, fwd.T])


def _odd_layer(x, pre_g, post_g, w_in, lam_re, lam_im, log_dt, b_re, b_im, c_re, c_im, d_skip,
               glu_w, glu_b, dw, ln_g, ln_b, pw, w_out, tl=32):
    nbatch, seq, d = x.shape
    rows = tl * nbatch
    bb, cc, lamr, lami = _s5_operands(lam_re, lam_im, log_dt, b_re, b_im, c_re, c_im, nbatch)
    lanes = S5_GROUPS * S5_STATE
    nsteps = seq // tl
    head_spec = pl.BlockSpec((nbatch, tl, d), lambda ss: (0, jnp.minimum(ss, nsteps - 1), 0))
    tail_spec = pl.BlockSpec((nbatch, tl, d), lambda ss: (0, jnp.maximum(ss - 1, 0), 0))
    ops = (x, x, _row_permutations(nbatch), pre_g.reshape(1, d), w_in.astype(BF16), bb, cc, lamr, lami,
           d_skip.reshape(1, S5_WIDTH),
           glu_w.astype(BF16), glu_b.reshape(1, 2 * S5_WIDTH),
           jnp.broadcast_to(dw.astype(F32)[:, None, :], (CONV_K, SUBLANES, CONV_WIDTH)), ln_g.reshape(1, CONV_WIDTH),
           ln_b.reshape(1, CONV_WIDTH), pw.astype(BF16), w_out.astype(BF16), post_g.reshape(1, d))
    return pl.pallas_call(
        functools.partial(_odd_kernel, nbatch=nbatch, tl=tl),
        grid=(nsteps + 1,),
        in_specs=[pl.BlockSpec(memory_space=pltpu.SMEM), head_spec, tail_spec] + [_const_spec(o.shape) for o in ops[2:]],
        out_specs=tail_spec,
        out_shape=jax.ShapeDtypeStruct(x.shape, x.dtype),
        scratch_shapes=[
            pltpu.VMEM((nbatch, lanes), F32),
            pltpu.VMEM((nbatch, lanes), F32),
            pltpu.VMEM((rows, d), BF16),
            pltpu.VMEM((rows, 2 * lanes), BF16),
            pltpu.VMEM((CONV_HIST * nbatch + 2 * rows + SUBLANES, CONV_WIDTH), F32),
            pltpu.VMEM(((3 + 2 * lanes // S5_SCAN_LANES + CONV_WIDTH // MXU_COLS - 1) * rows, S5_WIDTH), F32),
            pltpu.VMEM((rows, S5_WIDTH + CONV_WIDTH), BF16),
            pltpu.VMEM((nbatch, tl, S5_WIDTH + CONV_WIDTH), BF16),
            pltpu.VMEM((rows, d), F32),
        ],
        compiler_params=pltpu.CompilerParams(dimension_semantics=("arbitrary",),
                                             vmem_limit_bytes=VMEM_LIMIT_BYTES),
        name="odd_layer",
    )(jnp.zeros((1,), jnp.int32), *ops)


def kernel(x, pre_norm_g, post_norm_g, e_w_in, e_pool_w, e_pool_scale, e_w_out, o_w_in, o_lam_re, o_lam_im,
           o_log_dt, o_b_re, o_b_im, o_c_re, o_c_im, o_d, o_glu_w, o_glu_b, o_dw, o_ln_g, o_ln_b, o_pw, o_w_out):
    b, s, d = x.shape
    for i in range(pre_norm_g.shape[0]):
        j = i // 2
        if i % 2 == 0:
            x = _even_layer(x, pre_norm_g[i], post_norm_g[i], e_w_in[j], e_pool_w[j], e_pool_scale[j], e_w_out[j])
        else:
            x = _odd_layer(x, pre_norm_g[i], post_norm_g[i], o_w_in[j], o_lam_re[j], o_lam_im[j],
                           o_log_dt[j], o_b_re[j], o_b_im[j], o_c_re[j], o_c_im[j], o_d[j], o_glu_w[j],
                           o_glu_b[j], o_dw[j], o_ln_g[j], o_ln_b[j], o_pw[j], o_w_out[j])
    return x
```

```python
import functools
import math

import jax
import jax.numpy as jnp
from jax import lax
from jax.experimental import pallas as pl
from jax.experimental.pallas import tpu as pltpu

F32 = jnp.float32
BF16 = jnp.bfloat16

D_MODEL = 1024
EPS = 1e-6
NEG = -1e30

ATT_WIDTH = 512
HEAD_DIM = 128
ATT_HEADS = 4
ROT_DIM = 32
ROPE_THETA = 500000.0
MOBA_BLOCK = 256
MOBA_TOPK = 3
POOL_WIDTH = 512
POOL_WINDOWS = (2, 4, 8, 16)
POOL_GROUP_W = 128
POOL_CARRY = 16

S5_WIDTH = 512
S5_GROUP_IN = 16
S5_GROUPS = 32
S5_STATE = 64
S5_HALF_GROUPS = 16
S5_HALF_LANES = S5_HALF_GROUPS * S5_STATE
S5_SCAN_LANES = 512
CONV_WIDTH = 512
CONV_K = 31
CONV_HIST = 32
CONV_GROUP_TOKENS = 16
CONV_TAP_BLOCK = 8
CONV_MXU_SLACK = 5

SUBLANES = 8
LANES = 128
MXU_COLS = 256
PERM_ROWS = 256

V7X_VMEM_BYTES = 64 * 1024 * 1024
VMEM_LIMIT_BYTES = 60000 * 1024


def _rms(x, g):
    return x * lax.rsqrt(jnp.mean(x * x, axis=-1, keepdims=True) + EPS) * g


def _silu(x):
    return x * jax.nn.sigmoid(x)


def _dot(a, b):
    return jnp.dot(a, b, preferred_element_type=F32)


def _dot_nt(a, b, precision=None):
    return lax.dot_general(a, b, (((1,), (1,)), ((), ())), preferred_element_type=F32, precision=precision)


def _const_spec(shape):
    nd = len(shape)
    return pl.BlockSpec(shape, lambda *_: (0,) * nd, pipeline_mode=pl.Buffered(1))


def _rope(x, cosf, sina, sinb):
    return x * cosf + pltpu.roll(x, HEAD_DIM - ROT_DIM // 2, axis=1) * sina + pltpu.roll(x, ROT_DIM // 2, axis=1) * sinb


def _even_kernel(x_ref, xprev_ref, cos_ref, sina_ref, sinb_ref, pre_g_ref, w_in_ref, pool_w_ref, pool_s_ref, w_out_ref,
                 post_g_ref, o_ref, k_scr, vt_scr, kbar_scr, sel_scr, acc_scr, pcarry_scr, mix_scr,
                 *, n_steps, steps_per_seq):
    gstep = pl.program_id(0)
    step = lax.rem(jnp.minimum(gstep, n_steps - 1), steps_per_seq)
    tq = x_ref.shape[0]
    bs = MOBA_BLOCK
    nb = kbar_scr.shape[0]
    A = ATT_WIDTH
    first, second = slice(0, bs), slice(bs, 2 * bs)
    blk_a = 2 * step

    @pl.when(gstep == 0)
    def _():
        mix_scr[...] = jnp.zeros_like(mix_scr)

    @pl.when(step == 0)
    def _():
        kbar_scr[...] = jnp.zeros_like(kbar_scr)
        pcarry_scr[...] = jnp.zeros_like(pcarry_scr)

    y_prev = _dot(mix_scr[...], w_out_ref[...])
    o_ref[...] = xprev_ref[...] + _rms(y_prev, post_g_ref[...])

    xb = x_ref[...]
    h = _rms(xb, pre_g_ref[...]).astype(BF16)
    q = _dot(h, w_in_ref[:, 0:A])
    k = _dot(h, w_in_ref[:, A:2 * A])
    v = _dot(h, w_in_ref[:, 2 * A:3 * A])
    ga = _dot(h, w_in_ref[:, 3 * A:4 * A])
    pu = _dot(h, w_in_ref[:, 4 * A:4 * A + POOL_WIDTH])
    gb = _dot(h, w_in_ref[:, 4 * A + POOL_WIDTH:4 * A + 2 * POOL_WIDTH])

    cosf, sina, sinb = cos_ref[...], sina_ref[...], sinb_ref[...]
    hs = [slice(hh * HEAD_DIM, (hh + 1) * HEAD_DIM) for hh in range(ATT_HEADS)]
    q_r = [_rope(q[:, s], cosf, sina, sinb) for s in hs]
    k_r = jnp.concatenate([_rope(k[:, s], cosf, sina, sinb) for s in hs], axis=1)
    k_b = k_r.astype(BF16)
    vt_b = v.T.astype(BF16)

    blk = lax.broadcasted_iota(jnp.int32, (nb, 1), 0)
    kbar = jnp.where(blk == blk_a, jnp.mean(k_r[first, :], axis=0, keepdims=True),
                     jnp.where(blk == blk_a + 1, jnp.mean(k_r[second, :], axis=0, keepdims=True), kbar_scr[...]))
    kbar_scr[...] = kbar
    k_scr[blk_a] = k_b[first, :]
    k_scr[blk_a + 1] = k_b[second, :]
    vt_scr[blk_a] = vt_b[:, first]
    vt_scr[blk_a + 1] = vt_b[:, second]

    qblk = blk_a + jnp.where(lax.broadcasted_iota(jnp.int32, (1, tq), 1) >= bs, 1, 0)
    past = blk < qblk
    for hh, s in enumerate(hs):
        g = _dot_nt(kbar[:, s], q_r[hh], precision=lax.Precision.HIGHEST)
        g = jnp.where(past, g, -jnp.inf)
        rank = jnp.zeros(g.shape, F32)
        for m in range(nb):
            gm = g[m:m + 1, :]
            beats = (gm > g) | ((gm == g) & (m < blk))
            rank = rank + jnp.where(beats, 1.0, 0.0)
        sel_scr[hh] = jnp.where(past & (rank < MOBA_TOPK), 1.0, 0.0)

    ext = jnp.concatenate([pcarry_scr[...], pu], axis=0)
    pcarry_scr[...] = pu[tq - POOL_CARRY:, :]
    tpos = step * tq + lax.broadcasted_iota(jnp.int32, (tq, 1), 0)
    for gi, w in enumerate(POOL_WINDOWS):
        cs = slice(gi * POOL_GROUP_W, (gi + 1) * POOL_GROUP_W)
        cum = ext[:, cs]
        d = 1
        while d < w:
            cum = cum + pltpu.roll(cum, d, axis=0)
            d *= 2
        cnt = jnp.minimum(tpos + 1, w).astype(F32)
        mg = cum[POOL_CARRY:, :] / cnt - pu[:, cs]
        yg = _dot(mg.astype(BF16), pool_w_ref[gi]) * pool_s_ref[:, cs] * _silu(gb[:, cs])
        mix_scr[:, A + gi * POOL_GROUP_W:A + (gi + 1) * POOL_GROUP_W] = yg.astype(BF16)

    qscale = math.log2(math.e) / math.sqrt(HEAD_DIM)
    qb = [(qh * qscale).astype(BF16) for qh in q_r]
    gate_a = [_silu(ga[:, s]) for s in hs]
    key_i = lax.broadcasted_iota(jnp.int32, (bs, bs), 0)
    qry_i = lax.broadcasted_iota(jnp.int32, (bs, bs), 1)
    causal = key_i <= qry_i
    nh = range(ATT_HEADS)

    sel_ab = [sel_scr[hh, pl.ds(blk_a, 1), second] > 0.5 for hh in nh]
    t_a = [_dot_nt(k_b[first, hs[hh]], qb[hh]) for hh in nh]
    t_aa = [jnp.where(causal, t_a[hh][:, first], NEG) for hh in nh]
    t_ab = [jnp.where(sel_ab[hh], t_a[hh][:, second], NEG) for hh in nh]
    t_bb = [jnp.where(causal, _dot_nt(k_b[second, hs[hh]], qb[hh][second, :]), NEG) for hh in nh]
    m_a = [jnp.max(t_aa[hh], axis=0, keepdims=True) for hh in nh]
    m_b = [jnp.maximum(jnp.max(t_ab[hh], axis=0, keepdims=True), jnp.max(t_bb[hh], axis=0, keepdims=True)) for hh in nh]
    p_aa = [jnp.exp2(t_aa[hh] - m_a[hh]) for hh in nh]
    p_ab = [jnp.exp2(t_ab[hh] - m_b[hh]) for hh in nh]
    p_bb = [jnp.exp2(t_bb[hh] - m_b[hh]) for hh in nh]
    m0 = [jnp.concatenate([m_a[hh], m_b[hh]], axis=1) for hh in nh]
    l0 = [jnp.concatenate([jnp.sum(p_aa[hh], axis=0, keepdims=True),
                           jnp.sum(p_ab[hh], axis=0, keepdims=True) + jnp.sum(p_bb[hh], axis=0, keepdims=True)],
                          axis=1) for hh in nh]
    for hh in nh:
        acc_scr[hh, :, first] = _dot(vt_b[hs[hh], first], p_aa[hh].astype(BF16))
        acc_scr[hh, :, second] = _dot(vt_b[hs[hh], :],
                                      jnp.concatenate([p_ab[hh].astype(BF16), p_bb[hh].astype(BF16)], axis=0))

    def past_pair(jp, carry):
        m_run, l_run = carry
        j0 = 2 * jp
        ch = [(hh, j0 + d) for d in range(2) for hh in nh]
        both = [_dot_nt(k_scr[pl.ds(j0, 2), :, hs[hh]].reshape(2 * bs, HEAD_DIM), qb[hh]) for hh in nh]
        st = [both[hh][d * bs:(d + 1) * bs, :] for d in range(2) for hh in nh]
        st = [jnp.where(sel_scr[hh, pl.ds(j, 1), :] > 0.5, st[c], NEG) for c, (hh, j) in enumerate(ch)]
        cmax = [jnp.max(t, axis=0, keepdims=True) for t in st]
        m_new = [jnp.maximum(m_run[hh], jnp.maximum(cmax[hh], cmax[ATT_HEADS + hh])) for hh in nh]
        alpha = [jnp.exp2(m_run[hh] - m_new[hh]) for hh in nh]
        p = [jnp.exp2(st[c] - m_new[hh]) for c, (hh, j) in enumerate(ch)]
        csum = [jnp.sum(t, axis=0, keepdims=True) for t in p]
        l_new = [alpha[hh] * l_run[hh] + csum[hh] + csum[ATT_HEADS + hh] for hh in nh]
        pv = [_dot(jnp.concatenate([vt_scr[j0, hs[hh], :], vt_scr[j0 + 1, hs[hh], :]], axis=1),
                   jnp.concatenate([p[hh].astype(BF16), p[ATT_HEADS + hh].astype(BF16)], axis=0)) for hh in nh]
        for hh in nh:
            acc_scr[hh] = alpha[hh] * acc_scr[hh] + pv[hh]
        return tuple(m_new), tuple(l_new)

    _, l_f = lax.fori_loop(0, step, past_pair, (tuple(m0), tuple(l0)))
    for hh in nh:
        att = (acc_scr[hh] * (1.0 / l_f[hh])).T
        mix_scr[:, hs[hh]] = (att * gate_a[hh]).astype(BF16)


def _rope_tables(seq):
    pos = jnp.arange(seq, dtype=F32)
    inv = jnp.power(ROPE_THETA, -jnp.arange(0, ROT_DIM, 2, dtype=F32) / ROT_DIM)
    ang = pos[:, None] * inv[None, :]
    cos, sin = jnp.cos(ang), jnp.sin(ang)
    half = ROT_DIM // 2
    zeros = jnp.zeros((seq, HEAD_DIM - ROT_DIM), F32)
    zh = jnp.zeros((seq, half), F32)
    cosf = jnp.concatenate([cos, cos, jnp.ones((seq, HEAD_DIM - ROT_DIM), F32)], axis=1)
    sina = jnp.concatenate([-sin, zh, zeros], axis=1)
    sinb = jnp.concatenate([zh, sin, zeros], axis=1)
    return cosf, sina, sinb


def _even_layer(x, pre_g, post_g, w_in, pool_w, pool_scale, w_out):
    b, s, d = x.shape
    bs = MOBA_BLOCK
    tq = 2 * bs
    nb = s // bs
    cosf, sina, sinb = _rope_tables(s)
    spb = s // tq
    n_steps = b * spb

    def head_idx(g):
        return jnp.minimum(g, n_steps - 1)

    def tail_idx(g):
        return jnp.maximum(g - 1, 0)

    tab_spec = pl.BlockSpec((tq, HEAD_DIM), lambda g: (lax.rem(head_idx(g), spb), 0))
    head_spec = pl.BlockSpec((None, tq, d), lambda g: (head_idx(g) // spb, lax.rem(head_idx(g), spb), 0))
    tail_spec = pl.BlockSpec((None, tq, d), lambda g: (tail_idx(g) // spb, lax.rem(tail_idx(g), spb), 0))
    return pl.pallas_call(
        functools.partial(_even_kernel, n_steps=n_steps, steps_per_seq=spb),
        grid=(n_steps + 1,),
        in_specs=[head_spec, tail_spec, tab_spec, tab_spec, tab_spec,
                  _const_spec((1, d)), _const_spec(w_in.shape), _const_spec(pool_w.shape),
                  _const_spec((1, POOL_WIDTH)), _const_spec(w_out.shape), _const_spec((1, d))],
        out_specs=tail_spec,
        out_shape=jax.ShapeDtypeStruct(x.shape, x.dtype),
        scratch_shapes=[
            pltpu.VMEM((nb, bs, ATT_WIDTH), BF16),
            pltpu.VMEM((nb, ATT_WIDTH, bs), BF16),
            pltpu.VMEM((nb, ATT_WIDTH), F32),
            pltpu.VMEM((ATT_HEADS, nb, tq), F32),
            pltpu.VMEM((ATT_HEADS, HEAD_DIM, tq), F32),
            pltpu.VMEM((POOL_CARRY, POOL_WIDTH), F32),
            pltpu.VMEM((tq, ATT_WIDTH + POOL_WIDTH), BF16),
        ],
        compiler_params=pltpu.CompilerParams(dimension_semantics=("arbitrary",),
                                             vmem_limit_bytes=VMEM_LIMIT_BYTES),
        name="even_layer",
    )(x, x, cosf, sina, sinb, pre_g.reshape(1, d), w_in.astype(BF16), pool_w.astype(BF16),
      pool_scale.reshape(1, POOL_WIDTH), w_out.astype(BF16), post_g.reshape(1, d))


def _odd_kernel(zero_ref, x_ref, xprev_ref, perm_ref, pre_g_ref, w_in_ref, bb_ref, cc_ref, lamr_ref, lami_ref, d_ref,
                glu_w_ref, glu_b_ref, dwb_ref, ln_g_ref, ln_b_ref, pw_ref, w_out_ref, post_g_ref, o_ref,
                st_re, st_im, h_scr, xs_scr, work_scr, keep_scr, mix_scr, mixbt_scr, yo_scr, *, nbatch, tl):
    step = pl.program_id(0)
    rows = tl * nbatch
    hist = CONV_HIST * nbatch
    W = S5_WIDTH
    hw = 2 * S5_HALF_LANES
    half_in = S5_HALF_GROUPS * S5_GROUP_IN
    mc = MXU_COLS
    scan_chunks = S5_HALF_LANES // S5_SCAN_LANES

    gext0 = 0
    conv0 = gext0 + hist + rows
    spare0 = conv0 + rows
    su0, gc0, gd0, bu0 = 0, rows, 2 * rows, 3 * rows
    gl0 = bu0 + 4 * scan_chunks * rows

    def region(r0):
        return slice(r0, r0 + rows)

    @pl.when(step == 0)
    def _():
        st_re[...] = jnp.zeros_like(st_re)
        st_im[...] = jnp.zeros_like(st_im)
        work_scr[gext0:gext0 + hist, :] = jnp.zeros((hist, W), F32)
        mixbt_scr[...] = jnp.zeros_like(mixbt_scr)

    d_model = x_ref.shape[2]
    mix_prev = mixbt_scr[...].reshape(rows, W + CONV_WIDTH)

    tok_per_perm = PERM_ROWS // nbatch
    xb = x_ref[...].reshape(rows, d_model)
    h_bt = _rms(xb, pre_g_ref[...]).astype(BF16).reshape(nbatch, tl, d_model)
    for u in range(tl // tok_per_perm):
        sub = h_bt[:, u * tok_per_perm:(u + 1) * tok_per_perm, :].reshape(PERM_ROWS, d_model)
        h_scr[u * PERM_ROWS:(u + 1) * PERM_ROWS, :] = _dot(perm_ref[0], sub).astype(BF16)

    def finish(res):
        work_scr[spare0:spare0 + SUBLANES, 0:LANES] = res[rows - SUBLANES:, res.shape[1] - LANES:]

    def project(r0, c, col):
        def run():
            res = _dot(h_scr[...], w_in_ref[:, col:col + mc])
            keep_scr[region(r0), c * mc:(c + 1) * mc] = res
            finish(res)
        return run

    def s5_in(hf, c):
        def run():
            su_b = keep_scr[region(su0), hf * half_in:(hf + 1) * half_in].astype(BF16)
            res = _dot(su_b, bb_ref[hf, :, c * S5_SCAN_LANES:(c + 1) * S5_SCAN_LANES])
            keep_scr[region(bu0 + (hf * 2 * scan_chunks + c) * rows), :] = res
            finish(res)
        return run

    glu_chunks = CONV_WIDTH // mc
    mxu_work = []
    for c in range(1, glu_chunks):
        mxu_work += [project(gl0 + (c - 1) * rows, 0, 2 * W + c * mc),
                     project(gl0 + (c - 1) * rows, 1, 2 * W + CONV_WIDTH + c * mc)]
    mxu_work += [project(su0, c, c * mc) for c in range(W // mc)]
    mxu_work += [s5_in(hf, c) for hf in range(2) for c in range(2 * scan_chunks)]
    mxu_work += [project(gc0, c, W + c * mc) for c in range(W // mc)]
    mxu_work += [project(gd0, c, 2 * W + 2 * CONV_WIDTH + c * mc) for c in range(CONV_WIDTH // mc)]

    def finish_prev(c):
        def run():
            res = _dot(mix_prev, w_out_ref[:, c * mc:(c + 1) * mc])
            yo_scr[:, c * mc:(c + 1) * mc] = res
            finish(res)
        return run

    mxu_work += [finish_prev(c) for c in range(d_model // mc)]

    ca = _dot(h_scr[...], w_in_ref[:, 2 * W:2 * W + mc])
    cb = _dot(h_scr[...], w_in_ref[:, 2 * W + CONV_WIDTH:2 * W + CONV_WIDTH + mc])
    work_scr[gext0 + hist:gext0 + hist + rows, 0:mc] = ca * jax.nn.sigmoid(cb)
    lead = CONV_HIST - (CONV_K - 1)
    tiles_per_tok = nbatch // SUBLANES
    ng = CONV_GROUP_TOKENS
    zrow = pl.multiple_of(zero_ref[0] * SUBLANES, SUBLANES)
    n_groups = (CONV_WIDTH // LANES) * tiles_per_tok * (tl // ng)
    n_work = len(mxu_work)
    done = 0
    for lt in range(CONV_WIDTH // LANES):
        ls = slice(lt * LANES, (lt + 1) * LANES)
        if lt * LANES % mc == 0 and lt > 0:
            c = lt * LANES // mc
            while len(mxu_work) > n_work - 2 * c:
                mxu_work.pop(0)()
            glu = keep_scr[region(gl0 + (c - 1) * rows), :]
            work_scr[gext0 + hist:gext0 + hist + rows, c * mc:(c + 1) * mc] = glu[:, 0:mc] * jax.nn.sigmoid(glu[:, mc:2 * mc])
        for par in range(tiles_per_tok):
            for t0 in range(0, tl, ng):
                acc = [jnp.zeros((SUBLANES, LANES), F32) for _ in range(ng)]
                for k0 in range(0, CONV_K, CONV_TAP_BLOCK):
                    kn = min(CONV_TAP_BLOCK, CONV_K - k0)
                    taps = [dwb_ref[k0 + kk, :, ls] for kk in range(kn)]
                    for u in range(ng + kn - 1):
                        r0 = gext0 + ((t0 + lead + k0 + u) * tiles_per_tok + par) * SUBLANES
                        xin = work_scr[pl.ds(r0 + zrow, SUBLANES), ls]
                        for m in range(max(0, u - kn + 1), min(ng, u + 1)):
                            acc[m] = acc[m] + taps[u - m] * xin
                for m in range(ng):
                    r0 = conv0 + ((t0 + m) * tiles_per_tok + par) * SUBLANES
                    work_scr[r0:r0 + SUBLANES, ls] = acc[m]
                done += 1
                while len(mxu_work) > n_work - (max(0, done - CONV_MXU_SLACK) * n_work) // (n_groups - CONV_MXU_SLACK):
                    mxu_work.pop(0)()
    while mxu_work:
        mxu_work.pop(0)()
    work_scr[gext0:gext0 + hist, :] = work_scr[gext0 + rows:gext0 + rows + hist, :]
    conv = work_scr[region(conv0), :]
    mu = jnp.mean(conv, axis=-1, keepdims=True)
    cen = conv - mu
    var = jnp.mean(cen * cen, axis=-1, keepdims=True)
    nrm = cen * lax.rsqrt(var + EPS) * ln_g_ref[...] + ln_b_ref[...]
    mix_scr[:, W:W + CONV_WIDTH] = (_dot(_silu(nrm).astype(BF16), pw_ref[...])
                                    * _silu(keep_scr[region(gd0), :])).astype(BF16)

    y_half = []
    for c in range(2 * scan_chunks):
        hf, sub = divmod(c, scan_chunks)
        cl = slice(c * S5_SCAN_LANES, (c + 1) * S5_SCAN_LANES)
        re = slice(hf * hw + sub * S5_SCAN_LANES, hf * hw + (sub + 1) * S5_SCAN_LANES)
        im = slice(re.start + S5_HALF_LANES, re.stop + S5_HALF_LANES)
        bu_re = bu0 + (hf * 2 * scan_chunks + sub) * rows
        bu_im = bu_re + scan_chunks * rows
        ar, ai = lamr_ref[:, cl], lami_ref[:, cl]
        sr, si = st_re[:, cl], st_im[:, cl]
        for t in range(tl):
            rs = slice(t * nbatch, (t + 1) * nbatch)
            br = keep_scr[bu_re + t * nbatch:bu_re + (t + 1) * nbatch, :]
            bi = keep_scr[bu_im + t * nbatch:bu_im + (t + 1) * nbatch, :]
            sr, si = ar * sr - ai * si + br, ar * si + ai * sr + bi
            xs_scr[rs, re] = sr.astype(BF16)
            xs_scr[rs, im] = si.astype(BF16)
        st_re[:, cl] = sr
        st_im[:, cl] = si
        if sub == scan_chunks - 1:
            y_half.append(_dot(xs_scr[:, hf * hw:(hf + 1) * hw], cc_ref[hf]))

    o_ref[...] = (xprev_ref[...].reshape(rows, d_model) + _rms(yo_scr[...], post_g_ref[...])).reshape(nbatch, tl, d_model)

    y = jnp.concatenate(y_half, axis=1) + d_ref[...] * keep_scr[region(su0), :]
    yy = _dot(y.astype(BF16), glu_w_ref[...]) + glu_b_ref[...]
    mix_scr[:, 0:W] = (yy[:, 0:W] * jax.nn.sigmoid(yy[:, W:2 * W]) * _silu(keep_scr[region(gc0), :])).astype(BF16)

    for u in range(tl // tok_per_perm):
        back = _dot(perm_ref[1], mix_scr[u * PERM_ROWS:(u + 1) * PERM_ROWS, :]).astype(BF16)
        mixbt_scr[:, u * tok_per_perm:(u + 1) * tok_per_perm, :] = back.reshape(nbatch, tok_per_perm, W + CONV_WIDTH)


def _s5_operands(lam_re, lam_im, log_dt, b_re, b_im, c_re, c_im, nbatch):
    lr, li = lam_re.astype(F32), lam_im.astype(F32)
    dt = jnp.exp(log_dt.astype(F32))[:, None]
    mag = jnp.exp(lr * dt)
    ab_re = mag * jnp.cos(li * dt)
    ab_im = mag * jnp.sin(li * dt)
    den = lr * lr + li * li
    num_re = ab_re - 1.0
    f_re = (num_re * lr + ab_im * li) / den
    f_im = (ab_im * lr - num_re * li) / den
    br, bi = b_re.astype(F32), b_im.astype(F32)
    bb_re = f_re[..., None] * br - f_im[..., None] * bi
    bb_im = f_re[..., None] * bi + f_im[..., None] * br
    hg, p, n_in = S5_HALF_GROUPS, S5_STATE, S5_GROUP_IN
    eye = jnp.eye(hg, dtype=F32)

    def pack_b(m):
        return jnp.einsum('hgpj,gk->hgjkp', m.reshape(2, hg, p, n_in), eye).reshape(2, hg * n_in, hg * p)

    def pack_c(m):
        return jnp.einsum('hgip,gk->hgpki', m.reshape(2, hg, n_in, p), eye).reshape(2, hg * p, hg * n_in)

    bb = jnp.concatenate([pack_b(bb_re), pack_b(bb_im)], axis=2).astype(BF16)
    cc = jnp.concatenate([pack_c(c_re.astype(F32)), -pack_c(c_im.astype(F32))], axis=1).astype(BF16)
    lanes = S5_GROUPS * S5_STATE
    lamr = jnp.broadcast_to(ab_re.reshape(1, lanes), (nbatch, lanes))
    lami = jnp.broadcast_to(ab_im.reshape(1, lanes), (nbatch, lanes))
    return bb, cc, lamr, lami


def _row_permutations(nbatch):
    tok = PERM_ROWS // nbatch
    dst = jnp.arange(PERM_ROWS)
    src = (dst % nbatch) * tok + dst // nbatch
    fwd = (src[:, None] == jnp.arange(PERM_ROWS)[None, :]).astype(BF16)
    return jnp.stack([fwd, fwd.T])


def _odd_layer(x, pre_g, post_g, w_in, lam_re, lam_im, log_dt, b_re, b_im, c_re, c_im, d_skip,
               glu_w, glu_b, dw, ln_g, ln_b, pw, w_out, tl=32):
    nbatch, seq, d = x.shape
    rows = tl * nbatch
    bb, cc, lamr, lami = _s5_operands(lam_re, lam_im, log_dt, b_re, b_im, c_re, c_im, nbatch)
    lanes = S5_GROUPS * S5_STATE
    nsteps = seq // tl
    head_spec = pl.BlockSpec((nbatch, tl, d), lambda ss: (0, jnp.minimum(ss, nsteps - 1), 0))
    tail_spec = pl.BlockSpec((nbatch, tl, d), lambda ss: (0, jnp.maximum(ss - 1, 0), 0))
    ops = (x, x, _row_permutations(nbatch), pre_g.reshape(1, d), w_in.astype(BF16), bb, cc, lamr, lami,
           d_skip.reshape(1, S5_WIDTH),
           glu_w.astype(BF16), glu_b.reshape(1, 2 * S5_WIDTH),
           jnp.broadcast_to(dw.astype(F32)[:, None, :], (CONV_K, SUBLANES, CONV_WIDTH)), ln_g.reshape(1, CONV_WIDTH),
           ln_b.reshape(1, CONV_WIDTH), pw.astype(BF16), w_out.astype(BF16), post_g.reshape(1, d))
    return pl.pallas_call(
        functools.partial(_odd_kernel, nbatch=nbatch, tl=tl),
        grid=(nsteps + 1,),
        in_specs=[pl.BlockSpec(memory_space=pltpu.SMEM), head_spec, tail_spec] + [_const_spec(o.shape) for o in ops[2:]],
        out_specs=tail_spec,
        out_shape=jax.ShapeDtypeStruct(x.shape, x.dtype),
        scratch_shapes=[
            pltpu.VMEM((nbatch, lanes), F32),
            pltpu.VMEM((nbatch, lanes), F32),
            pltpu.VMEM((rows, d), BF16),
            pltpu.VMEM((rows, 2 * lanes), BF16),
            pltpu.VMEM((CONV_HIST * nbatch + 2 * rows + SUBLANES, CONV_WIDTH), F32),
            pltpu.VMEM(((3 + 2 * lanes // S5_SCAN_LANES + CONV_WIDTH // MXU_COLS - 1) * rows, S5_WIDTH), F32),
            pltpu.VMEM((rows, S5_WIDTH + CONV_WIDTH), BF16),
            pltpu.VMEM((nbatch, tl, S5_WIDTH + CONV_WIDTH), BF16),
            pltpu.VMEM((rows, d), F32),
        ],
        compiler_params=pltpu.CompilerParams(dimension_semantics=("arbitrary",),
                                             vmem_limit_bytes=VMEM_LIMIT_BYTES),
        name="odd_layer",
    )(jnp.zeros((1,), jnp.int32), *ops)


def kernel(x, pre_norm_g, post_norm_g, e_w_in, e_pool_w, e_pool_scale, e_w_out, o_w_in, o_lam_re, o_lam_im,
           o_log_dt, o_b_re, o_b_im, o_c_re, o_c_im, o_d, o_glu_w, o_glu_b, o_dw, o_ln_g, o_ln_b, o_pw, o_w_out):
    b, s, d = x.shape
    for i in range(pre_norm_g.shape[0]):
        j = i // 2
        if i % 2 == 0:
            x = _even_layer(x, pre_norm_g[i], post_norm_g[i], e_w_in[j], e_pool_w[j], e_pool_scale[j], e_w_out[j])
        else:
            x = _odd_layer(x, pre_norm_g[i], post_norm_g[i], o_w_in[j], o_lam_re[j], o_lam_im[j],
                           o_log_dt[j], o_b_re[j], o_b_im[j], o_c_re[j], o_c_im[j], o_d[j], o_glu_w[j],
                           o_glu_b[j], o_dw[j], o_ln_g[j], o_ln_b[j], o_pw[j], o_w_out[j])
    return x
```

```python
import functools
import math

import jax
import jax.numpy as jnp
from jax import lax
from jax.experimental import pallas as pl
from jax.experimental.pallas import tpu as pltpu

F32 = jnp.float32
BF16 = jnp.bfloat16

D_MODEL = 1024
EPS = 1e-6
NEG = -1e30

ATT_WIDTH = 512
HEAD_DIM = 128
ATT_HEADS = 4
ROT_DIM = 32
ROPE_THETA = 500000.0
MOBA_BLOCK = 256
MOBA_TOPK = 3
POOL_WIDTH = 512
POOL_WINDOWS = (2, 4, 8, 16)
POOL_GROUP_W = 128
POOL_CARRY = 16

S5_WIDTH = 512
S5_GROUP_IN = 16
S5_GROUPS = 32
S5_STATE = 64
S5_HALF_GROUPS = 16
S5_HALF_LANES = S5_HALF_GROUPS * S5_STATE
S5_SCAN_LANES = 512
CONV_WIDTH = 512
CONV_K = 31
CONV_HIST = 32
CONV_GROUP_TOKENS = 16
CONV_TAP_BLOCK = 8
CONV_MXU_SLACK = 2

SUBLANES = 8
LANES = 128
MXU_COLS = 256
PERM_ROWS = 256

V7X_VMEM_BYTES = 64 * 1024 * 1024
VMEM_LIMIT_BYTES = 60000 * 1024


def _rms(x, g):
    return x * lax.rsqrt(jnp.mean(x * x, axis=-1, keepdims=True) + EPS) * g


def _silu(x):
    return x * jax.nn.sigmoid(x)


def _dot(a, b):
    return jnp.dot(a, b, preferred_element_type=F32)


def _dot_nt(a, b, precision=None):
    return lax.dot_general(a, b, (((1,), (1,)), ((), ())), preferred_element_type=F32, precision=precision)


def _const_spec(shape):
    nd = len(shape)
    return pl.BlockSpec(shape, lambda *_: (0,) * nd, pipeline_mode=pl.Buffered(1))


def _rope(x, cosf, sina, sinb):
    return x * cosf + pltpu.roll(x, HEAD_DIM - ROT_DIM // 2, axis=1) * sina + pltpu.roll(x, ROT_DIM // 2, axis=1) * sinb


def _even_kernel(x_ref, cos_ref, sina_ref, sinb_ref, pre_g_ref, w_in_ref, pool_w_ref, pool_s_ref, w_out_ref,
                 post_g_ref, o_ref, k_scr, vt_scr, kbar_scr, sel_scr, acc_scr, pcarry_scr, mix_scr):
    step = pl.program_id(1)
    tq = x_ref.shape[0]
    bs = MOBA_BLOCK
    nb = kbar_scr.shape[0]
    A = ATT_WIDTH
    first, second = slice(0, bs), slice(bs, 2 * bs)
    blk_a = 2 * step

    @pl.when(step == 0)
    def _():
        kbar_scr[...] = jnp.zeros_like(kbar_scr)
        pcarry_scr[...] = jnp.zeros_like(pcarry_scr)

    xb = x_ref[...]
    h = _rms(xb, pre_g_ref[...]).astype(BF16)
    q = _dot(h, w_in_ref[:, 0:A])
    k = _dot(h, w_in_ref[:, A:2 * A])
    v = _dot(h, w_in_ref[:, 2 * A:3 * A])
    ga = _dot(h, w_in_ref[:, 3 * A:4 * A])
    pu = _dot(h, w_in_ref[:, 4 * A:4 * A + POOL_WIDTH])
    gb = _dot(h, w_in_ref[:, 4 * A + POOL_WIDTH:4 * A + 2 * POOL_WIDTH])

    cosf, sina, sinb = cos_ref[...], sina_ref[...], sinb_ref[...]
    hs = [slice(hh * HEAD_DIM, (hh + 1) * HEAD_DIM) for hh in range(ATT_HEADS)]
    q_r = [_rope(q[:, s], cosf, sina, sinb) for s in hs]
    k_r = jnp.concatenate([_rope(k[:, s], cosf, sina, sinb) for s in hs], axis=1)
    k_b = k_r.astype(BF16)
    vt_b = v.T.astype(BF16)

    blk = lax.broadcasted_iota(jnp.int32, (nb, 1), 0)
    kbar = jnp.where(blk == blk_a, jnp.mean(k_r[first, :], axis=0, keepdims=True),
                     jnp.where(blk == blk_a + 1, jnp.mean(k_r[second, :], axis=0, keepdims=True), kbar_scr[...]))
    kbar_scr[...] = kbar
    k_scr[blk_a] = k_b[first, :]
    k_scr[blk_a + 1] = k_b[second, :]
    vt_scr[blk_a] = vt_b[:, first]
    vt_scr[blk_a + 1] = vt_b[:, second]

    qblk = blk_a + jnp.where(lax.broadcasted_iota(jnp.int32, (1, tq), 1) >= bs, 1, 0)
    past = blk < qblk
    for hh, s in enumerate(hs):
        g = _dot_nt(kbar[:, s], q_r[hh], precision=lax.Precision.HIGHEST)
        g = jnp.where(past, g, -jnp.inf)
        rank = jnp.zeros(g.shape, F32)
        for m in range(nb):
            gm = g[m:m + 1, :]
            beats = (gm > g) | ((gm == g) & (m < blk))
            rank = rank + jnp.where(beats, 1.0, 0.0)
        sel_scr[hh] = jnp.where(past & (rank < MOBA_TOPK), 1.0, 0.0)

    ext = jnp.concatenate([pcarry_scr[...], pu], axis=0)
    pcarry_scr[...] = pu[tq - POOL_CARRY:, :]
    tpos = step * tq + lax.broadcasted_iota(jnp.int32, (tq, 1), 0)
    for gi, w in enumerate(POOL_WINDOWS):
        cs = slice(gi * POOL_GROUP_W, (gi + 1) * POOL_GROUP_W)
        cum = ext[:, cs]
        d = 1
        while d < w:
            cum = cum + pltpu.roll(cum, d, axis=0)
            d *= 2
        cnt = jnp.minimum(tpos + 1, w).astype(F32)
        mg = cum[POOL_CARRY:, :] / cnt - pu[:, cs]
        yg = _dot(mg.astype(BF16), pool_w_ref[gi]) * pool_s_ref[:, cs] * _silu(gb[:, cs])
        mix_scr[:, A + gi * POOL_GROUP_W:A + (gi + 1) * POOL_GROUP_W] = yg.astype(BF16)

    qscale = math.log2(math.e) / math.sqrt(HEAD_DIM)
    qb = [(qh * qscale).astype(BF16) for qh in q_r]
    gate_a = [_silu(ga[:, s]) for s in hs]
    key_i = lax.broadcasted_iota(jnp.int32, (bs, bs), 0)
    qry_i = lax.broadcasted_iota(jnp.int32, (bs, bs), 1)
    causal = key_i <= qry_i
    nh = range(ATT_HEADS)

    sel_ab = [sel_scr[hh, pl.ds(blk_a, 1), second] > 0.5 for hh in nh]
    t_a = [_dot_nt(k_b[first, hs[hh]], qb[hh]) for hh in nh]
    t_aa = [jnp.where(causal, t_a[hh][:, first], NEG) for hh in nh]
    t_ab = [jnp.where(sel_ab[hh], t_a[hh][:, second], NEG) for hh in nh]
    t_bb = [jnp.where(causal, _dot_nt(k_b[second, hs[hh]], qb[hh][second, :]), NEG) for hh in nh]
    m_a = [jnp.max(t_aa[hh], axis=0, keepdims=True) for hh in nh]
    m_b = [jnp.maximum(jnp.max(t_ab[hh], axis=0, keepdims=True), jnp.max(t_bb[hh], axis=0, keepdims=True)) for hh in nh]
    p_aa = [jnp.exp2(t_aa[hh] - m_a[hh]) for hh in nh]
    p_ab = [jnp.exp2(t_ab[hh] - m_b[hh]) for hh in nh]
    p_bb = [jnp.exp2(t_bb[hh] - m_b[hh]) for hh in nh]
    m0 = [jnp.concatenate([m_a[hh], m_b[hh]], axis=1) for hh in nh]
    l0 = [jnp.concatenate([jnp.sum(p_aa[hh], axis=0, keepdims=True),
                           jnp.sum(p_ab[hh], axis=0, keepdims=True) + jnp.sum(p_bb[hh], axis=0, keepdims=True)],
                          axis=1) for hh in nh]
    for hh in nh:
        acc_scr[hh, :, first] = _dot(vt_b[hs[hh], first], p_aa[hh].astype(BF16))
        acc_scr[hh, :, second] = _dot(vt_b[hs[hh], :],
                                      jnp.concatenate([p_ab[hh].astype(BF16), p_bb[hh].astype(BF16)], axis=0))

    def past_pair(jp, carry):
        m_run, l_run = carry
        j0 = 2 * jp
        ch = [(hh, j0 + d) for d in range(2) for hh in nh]
        both = [_dot_nt(k_scr[pl.ds(j0, 2), :, hs[hh]].reshape(2 * bs, HEAD_DIM), qb[hh]) for hh in nh]
        st = [both[hh][d * bs:(d + 1) * bs, :] for d in range(2) for hh in nh]
        st = [jnp.where(sel_scr[hh, pl.ds(j, 1), :] > 0.5, st[c], NEG) for c, (hh, j) in enumerate(ch)]
        cmax = [jnp.max(t, axis=0, keepdims=True) for t in st]
        m_new = [jnp.maximum(m_run[hh], jnp.maximum(cmax[hh], cmax[ATT_HEADS + hh])) for hh in nh]
        alpha = [jnp.exp2(m_run[hh] - m_new[hh]) for hh in nh]
        p = [jnp.exp2(st[c] - m_new[hh]) for c, (hh, j) in enumerate(ch)]
        csum = [jnp.sum(t, axis=0, keepdims=True) for t in p]
        l_new = [alpha[hh] * l_run[hh] + csum[hh] + csum[ATT_HEADS + hh] for hh in nh]
        pv = [_dot(jnp.concatenate([vt_scr[j0, hs[hh], :], vt_scr[j0 + 1, hs[hh], :]], axis=1),
                   jnp.concatenate([p[hh].astype(BF16), p[ATT_HEADS + hh].astype(BF16)], axis=0)) for hh in nh]
        for hh in nh:
            acc_scr[hh] = alpha[hh] * acc_scr[hh] + pv[hh]
        return tuple(m_new), tuple(l_new)

    _, l_f = lax.fori_loop(0, step, past_pair, (tuple(m0), tuple(l0)))
    for hh in nh:
        att = (acc_scr[hh] * (1.0 / l_f[hh])).T
        mix_scr[:, hs[hh]] = (att * gate_a[hh]).astype(BF16)

    y = _dot(mix_scr[...], w_out_ref[...])
    o_ref[...] = xb + _rms(y, post_g_ref[...])


def _rope_tables(seq):
    pos = jnp.arange(seq, dtype=F32)
    inv = jnp.power(ROPE_THETA, -jnp.arange(0, ROT_DIM, 2, dtype=F32) / ROT_DIM)
    ang = pos[:, None] * inv[None, :]
    cos, sin = jnp.cos(ang), jnp.sin(ang)
    half = ROT_DIM // 2
    zeros = jnp.zeros((seq, HEAD_DIM - ROT_DIM), F32)
    zh = jnp.zeros((seq, half), F32)
    cosf = jnp.concatenate([cos, cos, jnp.ones((seq, HEAD_DIM - ROT_DIM), F32)], axis=1)
    sina = jnp.concatenate([-sin, zh, zeros], axis=1)
    sinb = jnp.concatenate([zh, sin, zeros], axis=1)
    return cosf, sina, sinb


def _even_layer(x, pre_g, post_g, w_in, pool_w, pool_scale, w_out):
    b, s, d = x.shape
    bs = MOBA_BLOCK
    tq = 2 * bs
    nb = s // bs
    cosf, sina, sinb = _rope_tables(s)
    tab_spec = pl.BlockSpec((tq, HEAD_DIM), lambda bb, ii: (ii, 0))
    x_spec = pl.BlockSpec((None, tq, d), lambda bb, ii: (bb, ii, 0))
    return pl.pallas_call(
        _even_kernel,
        grid=(b, s // tq),
        in_specs=[x_spec, tab_spec, tab_spec, tab_spec,
                  _const_spec((1, d)), _const_spec(w_in.shape), _const_spec(pool_w.shape),
                  _const_spec((1, POOL_WIDTH)), _const_spec(w_out.shape), _const_spec((1, d))],
        out_specs=x_spec,
        out_shape=jax.ShapeDtypeStruct(x.shape, x.dtype),
        scratch_shapes=[
            pltpu.VMEM((nb, bs, ATT_WIDTH), BF16),
            pltpu.VMEM((nb, ATT_WIDTH, bs), BF16),
            pltpu.VMEM((nb, ATT_WIDTH), F32),
            pltpu.VMEM((ATT_HEADS, nb, tq), F32),
            pltpu.VMEM((ATT_HEADS, HEAD_DIM, tq), F32),
            pltpu.VMEM((POOL_CARRY, POOL_WIDTH), F32),
            pltpu.VMEM((tq, ATT_WIDTH + POOL_WIDTH), BF16),
        ],
        compiler_params=pltpu.CompilerParams(dimension_semantics=("arbitrary", "arbitrary"),
                                             vmem_limit_bytes=VMEM_LIMIT_BYTES),
        name="even_layer",
    )(x, cosf, sina, sinb, pre_g.reshape(1, d), w_in.astype(BF16), pool_w.astype(BF16),
      pool_scale.reshape(1, POOL_WIDTH), w_out.astype(BF16), post_g.reshape(1, d))


def _odd_kernel(zero_ref, x_ref, xprev_ref, perm_ref, pre_g_ref, w_in_ref, bb_ref, cc_ref, lamr_ref, lami_ref, d_ref,
                glu_w_ref, glu_b_ref, dwb_ref, ln_g_ref, ln_b_ref, pw_ref, w_out_ref, post_g_ref, o_ref,
                st_re, st_im, h_scr, xs_scr, work_scr, keep_scr, mix_scr, mixbt_scr, yo_scr, *, nbatch, tl):
    step = pl.program_id(0)
    rows = tl * nbatch
    hist = CONV_HIST * nbatch
    W = S5_WIDTH
    hw = 2 * S5_HALF_LANES
    half_in = S5_HALF_GROUPS * S5_GROUP_IN
    mc = MXU_COLS
    scan_chunks = S5_HALF_LANES // S5_SCAN_LANES

    gext0 = 0
    conv0 = gext0 + hist + rows
    spare0 = conv0 + rows
    su0, gc0, gd0, bu0 = 0, rows, 2 * rows, 3 * rows
    gl0 = bu0 + 4 * scan_chunks * rows

    def region(r0):
        return slice(r0, r0 + rows)

    @pl.when(step == 0)
    def _():
        st_re[...] = jnp.zeros_like(st_re)
        st_im[...] = jnp.zeros_like(st_im)
        work_scr[gext0:gext0 + hist, :] = jnp.zeros((hist, W), F32)
        mixbt_scr[...] = jnp.zeros_like(mixbt_scr)

    d_model = x_ref.shape[2]
    mix_prev = mixbt_scr[...].reshape(rows, W + CONV_WIDTH)

    tok_per_perm = PERM_ROWS // nbatch
    xb = x_ref[...].reshape(rows, d_model)
    h_bt = _rms(xb, pre_g_ref[...]).astype(BF16).reshape(nbatch, tl, d_model)
    for u in range(tl // tok_per_perm):
        sub = h_bt[:, u * tok_per_perm:(u + 1) * tok_per_perm, :].reshape(PERM_ROWS, d_model)
        h_scr[u * PERM_ROWS:(u + 1) * PERM_ROWS, :] = _dot(perm_ref[0], sub).astype(BF16)

    def finish(res):
        work_scr[spare0:spare0 + SUBLANES, 0:LANES] = res[rows - SUBLANES:, res.shape[1] - LANES:]

    def project(r0, c, col):
        def run():
            res = _dot(h_scr[...], w_in_ref[:, col:col + mc])
            keep_scr[region(r0), c * mc:(c + 1) * mc] = res
            finish(res)
        return run

    def s5_in(hf, c):
        def run():
            su_b = keep_scr[region(su0), hf * half_in:(hf + 1) * half_in].astype(BF16)
            res = _dot(su_b, bb_ref[hf, :, c * S5_SCAN_LANES:(c + 1) * S5_SCAN_LANES])
            keep_scr[region(bu0 + (hf * 2 * scan_chunks + c) * rows), :] = res
            finish(res)
        return run

    glu_chunks = CONV_WIDTH // mc
    mxu_work = []
    for c in range(1, glu_chunks):
        mxu_work += [project(gl0 + (c - 1) * rows, 0, 2 * W + c * mc),
                     project(gl0 + (c - 1) * rows, 1, 2 * W + CONV_WIDTH + c * mc)]
    mxu_work += [project(su0, c, c * mc) for c in range(W // mc)]
    mxu_work += [s5_in(hf, c) for hf in range(2) for c in range(2 * scan_chunks)]
    mxu_work += [project(gc0, c, W + c * mc) for c in range(W // mc)]
    mxu_work += [project(gd0, c, 2 * W + 2 * CONV_WIDTH + c * mc) for c in range(CONV_WIDTH // mc)]

    def finish_prev(c):
        def run():
            res = _dot(mix_prev, w_out_ref[:, c * mc:(c + 1) * mc])
            yo_scr[:, c * mc:(c + 1) * mc] = res
            finish(res)
        return run

    mxu_work += [finish_prev(c) for c in range(d_model // mc)]

    ca = _dot(h_scr[...], w_in_ref[:, 2 * W:2 * W + mc])
    cb = _dot(h_scr[...], w_in_ref[:, 2 * W + CONV_WIDTH:2 * W + CONV_WIDTH + mc])
    work_scr[gext0 + hist:gext0 + hist + rows, 0:mc] = ca * jax.nn.sigmoid(cb)
    lead = CONV_HIST - (CONV_K - 1)
    tiles_per_tok = nbatch // SUBLANES
    ng = CONV_GROUP_TOKENS
    zrow = pl.multiple_of(zero_ref[0] * SUBLANES, SUBLANES)
    n_groups = (CONV_WIDTH // LANES) * tiles_per_tok * (tl // ng)
    n_work = len(mxu_work)
    done = 0
    for lt in range(CONV_WIDTH // LANES):
        ls = slice(lt * LANES, (lt + 1) * LANES)
        if lt * LANES % mc == 0 and lt > 0:
            c = lt * LANES // mc
            while len(mxu_work) > n_work - 2 * c:
                mxu_work.pop(0)()
            glu = keep_scr[region(gl0 + (c - 1) * rows), :]
            work_scr[gext0 + hist:gext0 + hist + rows, c * mc:(c + 1) * mc] = glu[:, 0:mc] * jax.nn.sigmoid(glu[:, mc:2 * mc])
        for par in range(tiles_per_tok):
            for t0 in range(0, tl, ng):
                acc = [jnp.zeros((SUBLANES, LANES), F32) for _ in range(ng)]
                for k0 in range(0, CONV_K, CONV_TAP_BLOCK):
                    kn = min(CONV_TAP_BLOCK, CONV_K - k0)
                    taps = [dwb_ref[k0 + kk, :, ls] for kk in range(kn)]
                    for u in range(ng + kn - 1):
                        r0 = gext0 + ((t0 + lead + k0 + u) * tiles_per_tok + par) * SUBLANES
                        xin = work_scr[pl.ds(r0 + zrow, SUBLANES), ls]
                        for m in range(max(0, u - kn + 1), min(ng, u + 1)):
                            acc[m] = acc[m] + taps[u - m] * xin
                for m in range(ng):
                    r0 = conv0 + ((t0 + m) * tiles_per_tok + par) * SUBLANES
                    work_scr[r0:r0 + SUBLANES, ls] = acc[m]
                done += 1
                while len(mxu_work) > n_work - (max(0, done - CONV_MXU_SLACK) * n_work) // (n_groups - CONV_MXU_SLACK):
                    mxu_work.pop(0)()
    while mxu_work:
        mxu_work.pop(0)()
    work_scr[gext0:gext0 + hist, :] = work_scr[gext0 + rows:gext0 + rows + hist, :]
    conv = work_scr[region(conv0), :]
    mu = jnp.mean(conv, axis=-1, keepdims=True)
    cen = conv - mu
    var = jnp.mean(cen * cen, axis=-1, keepdims=True)
    nrm = cen * lax.rsqrt(var + EPS) * ln_g_ref[...] + ln_b_ref[...]
    mix_scr[:, W:W + CONV_WIDTH] = (_dot(_silu(nrm).astype(BF16), pw_ref[...])
                                    * _silu(keep_scr[region(gd0), :])).astype(BF16)

    y_half = []
    for c in range(2 * scan_chunks):
        hf, sub = divmod(c, scan_chunks)
        cl = slice(c * S5_SCAN_LANES, (c + 1) * S5_SCAN_LANES)
        re = slice(hf * hw + sub * S5_SCAN_LANES, hf * hw + (sub + 1) * S5_SCAN_LANES)
        im = slice(re.start + S5_HALF_LANES, re.stop + S5_HALF_LANES)
        bu_re = bu0 + (hf * 2 * scan_chunks + sub) * rows
        bu_im = bu_re + scan_chunks * rows
        ar, ai = lamr_ref[:, cl], lami_ref[:, cl]
        sr, si = st_re[:, cl], st_im[:, cl]
        for t in range(tl):
            rs = slice(t * nbatch, (t + 1) * nbatch)
            br = keep_scr[bu_re + t * nbatch:bu_re + (t + 1) * nbatch, :]
            bi = keep_scr[bu_im + t * nbatch:bu_im + (t + 1) * nbatch, :]
            sr, si = ar * sr - ai * si + br, ar * si + ai * sr + bi
            xs_scr[rs, re] = sr.astype(BF16)
            xs_scr[rs, im] = si.astype(BF16)
        st_re[:, cl] = sr
        st_im[:, cl] = si
        if sub == scan_chunks - 1:
            y_half.append(_dot(xs_scr[:, hf * hw:(hf + 1) * hw], cc_ref[hf]))

    o_ref[...] = (xprev_ref[...].reshape(rows, d_model) + _rms(yo_scr[...], post_g_ref[...])).reshape(nbatch, tl, d_model)

    y = jnp.concatenate(y_half, axis=1) + d_ref[...] * keep_scr[region(su0), :]
    yy = _dot(y.astype(BF16), glu_w_ref[...]) + glu_b_ref[...]
    mix_scr[:, 0:W] = (yy[:, 0:W] * jax.nn.sigmoid(yy[:, W:2 * W]) * _silu(keep_scr[region(gc0), :])).astype(BF16)

    for u in range(tl // tok_per_perm):
        back = _dot(perm_ref[1], mix_scr[u * PERM_ROWS:(u + 1) * PERM_ROWS, :]).astype(BF16)
        mixbt_scr[:, u * tok_per_perm:(u + 1) * tok_per_perm, :] = back.reshape(nbatch, tok_per_perm, W + CONV_WIDTH)


def _s5_operands(lam_re, lam_im, log_dt, b_re, b_im, c_re, c_im, nbatch):
    lr, li = lam_re.astype(F32), lam_im.astype(F32)
    dt = jnp.exp(log_dt.astype(F32))[:, None]
    mag = jnp.exp(lr * dt)
    ab_re = mag * jnp.cos(li * dt)
    ab_im = mag * jnp.sin(li * dt)
    den = lr * lr + li * li
    num_re = ab_re - 1.0
    f_re = (num_re * lr + ab_im * li) / den
    f_im = (ab_im * lr - num_re * li) / den
    br, bi = b_re.astype(F32), b_im.astype(F32)
    bb_re = f_re[..., None] * br - f_im[..., None] * bi
    bb_im = f_re[..., None] * bi + f_im[..., None] * br
    hg, p, n_in = S5_HALF_GROUPS, S5_STATE, S5_GROUP_IN
    eye = jnp.eye(hg, dtype=F32)

    def pack_b(m):
        return jnp.einsum('hgpj,gk->hgjkp', m.reshape(2, hg, p, n_in), eye).reshape(2, hg * n_in, hg * p)

    def pack_c(m):
        return jnp.einsum('hgip,gk->hgpki', m.reshape(2, hg, n_in, p), eye).reshape(2, hg * p, hg * n_in)

    bb = jnp.concatenate([pack_b(bb_re), pack_b(bb_im)], axis=2).astype(BF16)
    cc = jnp.concatenate([pack_c(c_re.astype(F32)), -pack_c(c_im.astype(F32))], axis=1).astype(BF16)
    lanes = S5_GROUPS * S5_STATE
    lamr = jnp.broadcast_to(ab_re.reshape(1, lanes), (nbatch, lanes))
    lami = jnp.broadcast_to(ab_im.reshape(1, lanes), (nbatch, lanes))
    return bb, cc, lamr, lami


def _row_permutations(nbatch):
    tok = PERM_ROWS // nbatch
    dst = jnp.arange(PERM_ROWS)
    src = (dst % nbatch) * tok + dst // nbatch
    fwd = (src[:, None] == jnp.arange(PERM_ROWS)[None, :]).astype(BF16)
    return jnp.stack([fwd, fwd.T])


def _odd_layer(x, pre_g, post_g, w_in, lam_re, lam_im, log_dt, b_re, b_im, c_re, c_im, d_skip,
               glu_w, glu_b, dw, ln_g, ln_b, pw, w_out, tl=32):
    nbatch, seq, d = x.shape
    rows = tl * nbatch
    bb, cc, lamr, lami = _s5_operands(lam_re, lam_im, log_dt, b_re, b_im, c_re, c_im, nbatch)
    lanes = S5_GROUPS * S5_STATE
    nsteps = seq // tl
    head_spec = pl.BlockSpec((nbatch, tl, d), lambda ss: (0, jnp.minimum(ss, nsteps - 1), 0))
    tail_spec = pl.BlockSpec((nbatch, tl, d), lambda ss: (0, jnp.maximum(ss - 1, 0), 0))
    ops = (x, x, _row_permutations(nbatch), pre_g.reshape(1, d), w_in.astype(BF16), bb, cc, lamr, lami,
           d_skip.reshape(1, S5_WIDTH),
           glu_w.astype(BF16), glu_b.reshape(1, 2 * S5_WIDTH),
           jnp.broadcast_to(dw.astype(F32)[:, None, :], (CONV_K, SUBLANES, CONV_WIDTH)), ln_g.reshape(1, CONV_WIDTH),
           ln_b.reshape(1, CONV_WIDTH), pw.astype(BF16), w_out.astype(BF16), post_g.reshape(1, d))
    return pl.pallas_call(
        functools.partial(_odd_kernel, nbatch=nbatch, tl=tl),
        grid=(nsteps + 1,),
        in_specs=[pl.BlockSpec(memory_space=pltpu.SMEM), head_spec, tail_spec] + [_const_spec(o.shape) for o in ops[2:]],
        out_specs=tail_spec,
        out_shape=jax.ShapeDtypeStruct(x.shape, x.dtype),
        scratch_shapes=[
            pltpu.VMEM((nbatch, lanes), F32),
            pltpu.VMEM((nbatch, lanes), F32),
            pltpu.VMEM((rows, d), BF16),
            pltpu.VMEM((rows, 2 * lanes), BF16),
            pltpu.VMEM((CONV_HIST * nbatch + 2 * rows + SUBLANES, CONV_WIDTH), F32),
            pltpu.VMEM(((3 + 2 * lanes // S5_SCAN_LANES + CONV_WIDTH // MXU_COLS - 1) * rows, S5_WIDTH), F32),
            pltpu.VMEM((rows, S5_WIDTH + CONV_WIDTH), BF16),
            pltpu.VMEM((nbatch, tl, S5_WIDTH + CONV_WIDTH), BF16),
            pltpu.VMEM((rows, d), F32),
        ],
        compiler_params=pltpu.CompilerParams(dimension_semantics=("arbitrary",),
                                             vmem_limit_bytes=VMEM_LIMIT_BYTES),
        name="odd_layer",
    )(jnp.zeros((1,), jnp.int32), *ops)


def kernel(x, pre_norm_g, post_norm_g, e_w_in, e_pool_w, e_pool_scale, e_w_out, o_w_in, o_lam_re, o_lam_im,
           o_log_dt, o_b_re, o_b_im, o_c_re, o_c_im, o_d, o_glu_w, o_glu_b, o_dw, o_ln_g, o_ln_b, o_pw, o_w_out):
    b, s, d = x.shape
    for i in range(pre_norm_g.shape[0]):
        j = i // 2
        if i % 2 == 0:
            x = _even_layer(x, pre_norm_g[i], post_norm_g[i], e_w_in[j], e_pool_w[j], e_pool_scale[j], e_w_out[j])
        else:
            x = _odd_layer(x, pre_norm_g[i], post_norm_g[i], o_w_in[j], o_lam_re[j], o_lam_im[j],
                           o_log_dt[j], o_b_re[j], o_b_im[j], o_c_re[j], o_c_im[j], o_d[j], o_glu_w[j],
                           o_glu_b[j], o_dw[j], o_ln_g[j], o_ln_b[j], o_pw[j], o_w_out[j])
    return x
```
